```python
import jax, jax.numpy as jnp
from jax import lax
import numpy as np

D_MODEL = 1024
BATCH = 16
SEQ = 2048
DEPTH = 1

REC_HEADS = D_MODEL // 256
REC_DK = 128
REC_DV = 128
REC_WIDTH = REC_HEADS * REC_DK
REC_VWIDTH = REC_HEADS * REC_DV
REC_CHUNK = 64
ATT_Q_HEADS = D_MODEL // 128
ATT_KV_HEADS = 2
ATT_HEAD_DIM = 64
ATT_QWIDTH = ATT_Q_HEADS * ATT_HEAD_DIM
ATT_KVWIDTH = ATT_KV_HEADS * ATT_HEAD_DIM
ATT_WINDOW = 128
ATT_BLOCK = 128
ROPE_THETA = 10000.0
MAX_POS_OFFSET = 512
D_FF = ((8 * D_MODEL // 3 + 127) // 128) * 128
CONV_K = 3
LN_EPS = 1e-5
RMS_EPS = 1e-6
DEEPNORM_ALPHA = (2 * DEPTH) ** 0.25
DEEPNORM_BETA = (8 * DEPTH) ** -0.25
SPLITS = [REC_WIDTH, REC_WIDTH, REC_WIDTH, REC_VWIDTH, REC_VWIDTH,
          ATT_QWIDTH, ATT_KVWIDTH, ATT_KVWIDTH, D_MODEL, D_MODEL]
D_IN = sum(SPLITS)

kernel_name = "hybrid_hgrn2_swa_sink_convffn_deepnorm_adaln"


def layer_norm(x, g, b):
    xf = x.astype(jnp.float32)
    mu = jnp.mean(xf, axis=-1, keepdims=True)
    xc = xf - mu
    var = jnp.mean(xc * xc, axis=-1, keepdims=True)
    return (xc * lax.rsqrt(var + LN_EPS) * g + b).astype(x.dtype)


def chunked_gated_scan(q, k, v, log_f):
    B, H, S, dk = q.shape
    dv = v.shape[-1]
    n = S // REC_CHUNK

    def to_chunks(t):
        return t.reshape(B, H, n, REC_CHUNK, t.shape[-1]).transpose(2, 0, 1, 3, 4)

    qc, kc, vc, gc = to_chunks(q), to_chunks(k), to_chunks(v), to_chunks(log_f)
    lower = jnp.tril(jnp.ones((REC_CHUNK, REC_CHUNK), dtype=bool))[:, :, None]

    def step(state, inp):
        qi, ki, vi, gi = inp
        b = jnp.cumsum(gi, axis=2)
        diff = b[:, :, :, None, :] - b[:, :, None, :, :]
        decay = jnp.exp(jnp.where(lower, diff, -jnp.inf))
        scores = jnp.einsum('bhtk,bhsk,bhtsk->bhts', qi, ki, decay)
        o = (jnp.einsum('bhts,bhsv->bhtv', scores, vi)
             + jnp.einsum('bhtk,bhkv->bhtv', qi * jnp.exp(b), state))
        b_last = b[:, :, -1:, :]
        state = (jnp.exp(b_last[:, :, 0, :])[..., None] * state
                 + jnp.einsum('bhsk,bhsv->bhkv', ki * jnp.exp(b_last - b), vi))
        return state, o

    state0 = jnp.zeros((B, H, dk, dv), jnp.float32)
    _, o = lax.scan(step, state0, (qc, kc, vc, gc))
    return o.transpose(1, 2, 0, 3, 4).reshape(B, H, S, dv)


def hgrn2_bidirectional(q_raw, f_fwd_raw, f_bwd_raw, i_raw, g_raw, lb_fwd, lb_bwd, norm_g):
    B, S, _ = q_raw.shape

    def heads(t, d):
        return t.reshape(B, S, REC_HEADS, d).transpose(0, 2, 1, 3).astype(jnp.float32)

    q = heads(jax.nn.silu(q_raw), REC_DK) * (REC_DK ** -0.5)
    v = heads(i_raw, REC_DV)

    def gates(f_raw, lb):
        lb = lb.astype(jnp.float32).reshape(REC_HEADS, 1, REC_DK)
        f = lb + (1.0 - lb) * jax.nn.sigmoid(heads(f_raw, REC_DK))
        return 1.0 - f, jnp.log(f)

    k_f, logf_f = gates(f_fwd_raw, lb_fwd)
    k_b, logf_b = gates(f_bwd_raw, lb_bwd)
    o_fwd = chunked_gated_scan(q, k_f, v, logf_f)
    flip = lambda t: jnp.flip(t, axis=2)
    o_bwd = flip(chunked_gated_scan(flip(q), flip(k_b), flip(v), flip(logf_b)))
    o = o_fwd + o_bwd
    o = o * lax.rsqrt(jnp.mean(o * o, axis=-1, keepdims=True) + RMS_EPS) * norm_g.astype(jnp.float32)
    o = o.transpose(0, 2, 1, 3).reshape(B, S, REC_VWIDTH).astype(g_raw.dtype)
    return o * jax.nn.silu(g_raw)


def rotary(t, positions):
    half = t.shape[-1] // 2
    inv_freq = ROPE_THETA ** (-jnp.arange(half, dtype=jnp.float32) / half)
    ang = positions.astype(jnp.float32)[..., None] * inv_freq
    cos = jnp.cos(ang)[:, :, None, :]
    sin = jnp.sin(ang)[:, :, None, :]
    t1 = t[..., :half].astype(jnp.float32)
    t2 = t[..., half:].astype(jnp.float32)
    return jnp.concatenate([t1 * cos - t2 * sin, t2 * cos + t1 * sin], axis=-1).astype(t.dtype)


def windowed_gqa_with_sink(q, k, v, sink):
    B, S, Hq, hd = q.shape
    Hkv = k.shape[2]
    G = Hq // Hkv
    nb = S // ATT_BLOCK
    qb = q.reshape(B, nb, ATT_BLOCK, Hkv, G, hd)

    def band(t):
        tp = jnp.pad(t, ((0, 0), (ATT_BLOCK, ATT_BLOCK), (0, 0), (0, 0)))
        tp = tp.reshape(B, nb + 2, ATT_BLOCK, Hkv, hd)
        return jnp.concatenate([tp[:, :-2], tp[:, 1:-1], tp[:, 2:]], axis=2)

    kb, vb = band(k), band(v)
    scores = jnp.einsum('bnqhgd,bnkhd->bnhgqk', qb, kb).astype(jnp.float32) * (hd ** -0.5)
    q_idx = jnp.arange(ATT_BLOCK)[:, None]
    k_idx = jnp.arange(3 * ATT_BLOCK)[None, :] - ATT_BLOCK
    k_abs = jnp.arange(nb)[:, None, None] * ATT_BLOCK + k_idx[None]
    valid = (jnp.abs(k_idx - q_idx)[None] <= ATT_WINDOW) & (k_abs >= 0) & (k_abs < S)
    scores = jnp.where(valid[None, :, None, None], scores, -jnp.inf)
    sink_logit = jnp.broadcast_to(sink.astype(jnp.float32).reshape(1, 1, Hkv, G, 1, 1),
                                  scores.shape[:-1] + (1,))
    probs = jax.nn.softmax(jnp.concatenate([scores, sink_logit], axis=-1), axis=-1)[..., :-1]
    out = jnp.einsum('bnhgqk,bnkhd->bnqhgd', probs.astype(v.dtype), vb)
    return out.reshape(B, S, Hq * hd)


def depthwise_conv_centred(h, w, b):
    pad = CONV_K // 2
    hp = jnp.pad(h, ((0, 0), (pad, pad), (0, 0)))
    S = h.shape[1]
    out = b
    for tap in range(CONV_K):
        out = out + hp[:, tap:tap + S] * w[tap]
    return out


def setup_inputs(seed: int = 0) -> dict:
    key = jax.random.key(seed)
    ks = jax.random.split(key, 21)
    nrm = lambda k, shape, scale: jax.random.normal(k, shape, jnp.float32) * scale
    x = nrm(ks[0], (BATCH, SEQ, D_MODEL), 1.0)
    c = nrm(ks[1], (BATCH, D_MODEL), 1.0)
    positions = (jnp.arange(SEQ, dtype=jnp.int32)[None, :]
                 + jax.random.randint(ks[2], (BATCH, 1), 0, MAX_POS_OFFSET, dtype=jnp.int32))
    w_ada = nrm(ks[3], (DEPTH, D_MODEL, 6 * D_MODEL), 0.5 * D_MODEL ** -0.5)
    b_ada = nrm(ks[4], (DEPTH, 6 * D_MODEL), 0.02)
    w_in = nrm(ks[5], (DEPTH, D_MODEL, D_IN), D_MODEL ** -0.5)
    rec_lower_bound = nrm(ks[6], (2, DEPTH + 1, REC_WIDTH), 1.0)
    rec_norm_g = 1.0 + nrm(ks[7], (DEPTH, REC_DV), 0.02)
    w_rec_branch = nrm(ks[8], (DEPTH, REC_VWIDTH, D_MODEL), DEEPNORM_BETA * REC_VWIDTH ** -0.5)
    attn_sink = nrm(ks[9], (DEPTH, ATT_Q_HEADS), 0.5)
    w_attn_branch = nrm(ks[10], (DEPTH, ATT_QWIDTH, D_MODEL), DEEPNORM_BETA * ATT_QWIDTH ** -0.5)
    w_out = nrm(ks[11], (DEPTH, D_MODEL, D_MODEL), DEEPNORM_BETA * D_MODEL ** -0.5)
    ln1_g = 1.0 + nrm(ks[12], (DEPTH, D_MODEL), 0.02)
    ln1_b = nrm(ks[13], (DEPTH, D_MODEL), 0.02)
    w_up = nrm(ks[14], (DEPTH, D_MODEL, 2 * D_FF), D_MODEL ** -0.5)
    conv_w = nrm(ks[15], (DEPTH, CONV_K, 2 * D_FF), CONV_K ** -0.5)
    conv_b = nrm(ks[16], (DEPTH, 2 * D_FF), 0.02)
    w_down = nrm(ks[17], (DEPTH, D_FF, D_MODEL), DEEPNORM_BETA * D_FF ** -0.5)
    ln2_g = 1.0 + nrm(ks[18], (DEPTH, D_MODEL), 0.02)
    ln2_b = nrm(ks[19], (DEPTH, D_MODEL), 0.02)
    return {"x": x, "c": c, "positions": positions, "w_ada": w_ada, "b_ada": b_ada,
            "w_in": w_in, "rec_lower_bound": rec_lower_bound, "rec_norm_g": rec_norm_g,
            "w_rec_branch": w_rec_branch, "attn_sink": attn_sink, "w_attn_branch": w_attn_branch,
            "w_out": w_out, "ln1_g": ln1_g, "ln1_b": ln1_b, "w_up": w_up, "conv_w": conv_w,
            "conv_b": conv_b, "w_down": w_down, "ln2_g": ln2_g, "ln2_b": ln2_b}


def reference(x, c, positions, w_ada, b_ada, w_in, rec_lower_bound, rec_norm_g, w_rec_branch,
              attn_sink, w_attn_branch, w_out, ln1_g, ln1_b, w_up, conv_w, conv_b, w_down,
              ln2_g, ln2_b):
    B, S, D = x.shape
    lb_all = jnp.cumsum(jax.nn.softmax(rec_lower_bound.astype(jnp.float32), axis=1), axis=1)
    split_points = [int(p) for p in np.cumsum(SPLITS)[:-1]]
    for l in range(DEPTH):
        mods = jax.nn.silu(c) @ w_ada[l] + b_ada[l]
        sh1, sc1, ga1, sh2, sc2, ga2 = [m[:, None, :] for m in jnp.split(mods, 6, axis=-1)]

        u = x * (1.0 + sc1) + sh1
        proj = u @ w_in[l]
        (rq, rff, rfb, ri, rg, aq, ak, av, gate_rec, gate_att) = jnp.split(proj, split_points, axis=-1)
        y_rec = hgrn2_bidirectional(rq, rff, rfb, ri, rg, lb_all[0, l], lb_all[1, l],
                                    rec_norm_g[l]) @ w_rec_branch[l]
        q = rotary(aq.reshape(B, S, ATT_Q_HEADS, ATT_HEAD_DIM), positions)
        k = rotary(ak.reshape(B, S, ATT_KV_HEADS, ATT_HEAD_DIM), positions)
        v = av.reshape(B, S, ATT_KV_HEADS, ATT_HEAD_DIM)
        y_att = windowed_gqa_with_sink(q, k, v, attn_sink[l]) @ w_attn_branch[l]
        merged = jax.nn.sigmoid(gate_rec) * y_rec + jax.nn.sigmoid(gate_att) * y_att
        x = layer_norm(DEEPNORM_ALPHA * x + (1.0 + ga1) * (merged @ w_out[l]), ln1_g[l], ln1_b[l])

        u = x * (1.0 + sc2) + sh2
        h = depthwise_conv_centred(u @ w_up[l], conv_w[l], conv_b[l])
        h_val, h_gate = jnp.split(h, 2, axis=-1)
        ffn = (jax.nn.gelu(h_gate) * h_val) @ w_down[l]
        x = layer_norm(DEEPNORM_ALPHA * x + (1.0 + ga2) * ffn, ln2_g[l], ln2_b[l])
    return x
```

```python
import functools

import numpy as np
import jax
import jax.numpy as jnp
from jax import lax
from jax.experimental import pallas as pl
from jax.experimental.pallas import tpu as pltpu

F32 = jnp.float32
BF16 = jnp.bfloat16

REC_HEADS = 4
REC_DK = 128
REC_DV = 128
REC_WIDTH = REC_HEADS * REC_DK
ATT_Q_HEADS = 8
ATT_KV_HEADS = 2
ATT_HEAD_DIM = 64
ATT_QWIDTH = ATT_Q_HEADS * ATT_HEAD_DIM
ATT_KVWIDTH = ATT_KV_HEADS * ATT_HEAD_DIM
ATT_WINDOW = 128
ATT_BLOCK = 128
ROPE_THETA = 10000.0
CONV_K = 3
LN_EPS = 1e-5
RMS_EPS = 1e-6
DEPTH = 1
DEEPNORM_ALPHA = (2 * DEPTH) ** 0.25

LANES = 128
REC_CHUNK = 64
REC_LEVELS = (32, 16, 8, 4, 2, 1)
NEG_BIG = -1e30
VMEM_LIMIT = 56 * 1024 * 1024


def _sigmoid(x):
    return 1.0 / (1.0 + jnp.exp(-x))


def _silu(x):
    return x * _sigmoid(x)


def _layer_norm(r, g, b):
    mu = jnp.mean(r, axis=-1, keepdims=True)
    rc = r - mu
    var = jnp.mean(rc * rc, axis=-1, keepdims=True)
    return rc * lax.rsqrt(var + LN_EPS) * g + b


def _ada_kernel(c_ref, w_ref, b_ref, o_ref):
    c = c_ref[...]
    o_ref[...] = jnp.dot(_silu(c), w_ref[...], preferred_element_type=F32) + b_ref[...]


def _ada(c, w_ada, b_ada):
    B, D = c.shape
    N = w_ada.shape[1]
    return pl.pallas_call(
        _ada_kernel,
        grid=(N // D,),
        in_specs=[pl.BlockSpec((B, D), lambda j: (0, 0)),
                  pl.BlockSpec((D, D), lambda j: (0, j)),
                  pl.BlockSpec((1, D), lambda j: (0, j))],
        out_specs=pl.BlockSpec((B, D), lambda j: (0, j)),
        out_shape=jax.ShapeDtypeStruct((B, N), F32),
        name="ada",
    )(c, w_ada, b_ada.reshape(1, N))


def _proj_kernel(x_ref, sc_ref, sh_ref, pos_ref, invf_ref, w_ref, rlb_ref,
                 q_o, kf_o, kb_o, lf_o, lb_o, v_o, g_o, aq_o, ak_o, av_o, gr_o, ga_o):
    u = (x_ref[0] * (1.0 + sc_ref[0]) + sh_ref[0]).astype(BF16)

    def mm(lo, width):
        return jnp.dot(u, w_ref[:, lo:lo + width], preferred_element_type=F32)

    W = REC_WIDTH
    q_o[0] = (_silu(mm(0, W)) * (REC_DK ** -0.5)).astype(BF16)

    def forget(col, direction, k_o, l_o):
        r = rlb_ref[direction]
        e = jnp.exp(r - jnp.max(r, axis=0, keepdims=True))
        lb = e[0:1] / jnp.sum(e, axis=0, keepdims=True)
        f = lb + (1.0 - lb) * _sigmoid(mm(col, W))
        k_o[0] = (1.0 - f).astype(BF16)
        l_o[0] = jnp.log(f)

    forget(W, 0, kf_o, lf_o)
    forget(2 * W, 1, kb_o, lb_o)
    v_o[0] = mm(3 * W, W).astype(BF16)
    g_o[0] = _silu(mm(4 * W, W)).astype(BF16)

    ang = pos_ref[0].astype(F32) * invf_ref[...]
    cos = jnp.cos(ang)
    sin = jnp.sin(ang)
    lane = lax.broadcasted_iota(jnp.int32, ang.shape, 1)
    first_half = (lane % ATT_HEAD_DIM) < (ATT_HEAD_DIM // 2)
    sin_signed = jnp.where(first_half, -sin, sin)

    def rope(t):
        partner = jnp.where(first_half,
                            pltpu.roll(t, LANES - ATT_HEAD_DIM // 2, axis=1),
                            pltpu.roll(t, ATT_HEAD_DIM // 2, axis=1))
        return t * cos + partner * sin_signed

    c0 = 5 * W
    aq = mm(c0, ATT_QWIDTH)
    for s in range(ATT_QWIDTH // LANES):
        sl = slice(s * LANES, (s + 1) * LANES)
        aq_o[0, :, sl] = (rope(aq[:, sl]) * (ATT_HEAD_DIM ** -0.5)).astype(BF16)
    c0 += ATT_QWIDTH
    ak_o[0] = rope(mm(c0, ATT_KVWIDTH)).astype(BF16)
    c0 += ATT_KVWIDTH
    av_o[0] = mm(c0, ATT_KVWIDTH).astype(BF16)
    c0 += ATT_KVWIDTH
    D = x_ref.shape[-1]
    gr_o[0] = _sigmoid(mm(c0, D)).astype(BF16)
    ga_o[0] = _sigmoid(mm(c0 + D, D)).astype(BF16)


def _proj(x, sc1, sh1, positions, w_in_bf, rec_lower_bound, tm):
    B, S, D = x.shape
    d_in = w_in_bf.shape[1]
    half = ATT_HEAD_DIM // 2
    inv_freq = ROPE_THETA ** (-jnp.arange(half, dtype=F32) / half)
    invf = jnp.tile(inv_freq, LANES // half).reshape(1, LANES)
    W = REC_WIDTH
    tok = lambda b, t: (b, t, 0)
    per_b = lambda b, t: (b, 0, 0)
    outs = [(W, BF16), (W, BF16), (W, BF16), (W, F32), (W, F32), (W, BF16), (W, BF16),
            (ATT_QWIDTH, BF16), (ATT_KVWIDTH, BF16), (ATT_KVWIDTH, BF16), (D, BF16), (D, BF16)]
    return pl.pallas_call(
        _proj_kernel,
        grid=(B, S // tm),
        in_specs=[pl.BlockSpec((1, tm, D), tok),
                  pl.BlockSpec((1, 1, D), per_b),
                  pl.BlockSpec((1, 1, D), per_b),
                  pl.BlockSpec((1, tm, 1), tok),
                  pl.BlockSpec((1, LANES), lambda b, t: (0, 0)),
                  pl.BlockSpec((D, d_in), lambda b, t: (0, 0), pipeline_mode=pl.Buffered(1)),
                  pl.BlockSpec(rec_lower_bound.shape, lambda b, t: (0, 0, 0))],
        out_specs=[pl.BlockSpec((1, tm, w), tok) for w, _ in outs],
        out_shape=[jax.ShapeDtypeStruct((B, S, w), dt) for w, dt in outs],
        compiler_params=pltpu.CompilerParams(vmem_limit_bytes=VMEM_LIMIT),
        name="proj",
    )(x, sc1, sh1, positions.reshape(B, S, 1), invf, w_in_bf, rec_lower_bound)


def _rec_constants():
    C = REC_CHUNK
    nblk = 2 + len(REC_LEVELS)
    mf = np.zeros((nblk, C, C), np.float32)
    r = np.arange(C)
    for t in range(C):
        mf[0, t] = r <= t
        mf[1, t] = r > t
        for i, m in enumerate(REC_LEVELS):
            boundary = (t // (2 * m)) * 2 * m + m
            if t >= boundary:
                mf[2 + i, t] = (r >= boundary) & (r <= t)
            else:
                mf[2 + i, t] = (r > t) & (r <= boundary - 1)
    mb = mf[:, ::-1, ::-1]
    x = r[:, None] ^ r[None, :]
    lev = np.where(x > 0, np.floor(np.log2(np.maximum(x, 1))).astype(np.int32), -1)
    invalid = -2
    levf = np.where(r[:, None] >= r[None, :], lev, invalid).astype(np.int32)
    levb = np.where(r[:, None] <= r[None, :], lev, invalid).astype(np.int32)
    m3 = lambda m: np.concatenate([m.reshape(nblk * C, C)] * 3, axis=1)
    return (jnp.asarray(np.stack([m3(mf), m3(mb)]), BF16),
            jnp.asarray(np.stack([levf, levb])))


def _rec_kernel(q_ref, kf_ref, kb_ref, lf_ref, lb_ref, v_ref, g_ref, ng_ref, m_ref, lev_ref,
                o_ref, acc_ref):
    C = REC_CHUNK
    S = q_ref.shape[1]
    n_chunks = S // C
    row = lax.broadcasted_iota(jnp.int32, (C, LANES), 0)
    nt = (((1,), (1,)), ((), ()))
    tn = (((0,), (0,)), ((), ()))

    def run(direction, k_ref, l_ref):
        m3 = m_ref[direction]
        lev = lev_ref[direction]
        last = C - 1 if direction == 0 else 0

        def chunk(i, st):
            n = i if direction == 0 else n_chunks - 1 - i
            r0 = pl.multiple_of(n * C, C)
            rows = pl.ds(r0, C)
            lg = l_ref[0, rows, :]
            hi = lg.astype(BF16)
            r1 = lg - hi.astype(F32)
            mid = r1.astype(BF16)
            lo = (r1 - mid.astype(F32)).astype(BF16)
            e = jnp.dot(m3, jnp.concatenate([hi, mid, lo], axis=0), preferred_element_type=F32)
            x = jnp.exp(e)
            q = q_ref[0, rows, :]
            k = k_ref[0, rows, :]
            v = v_ref[0, rows, :]
            qf = q.astype(F32)
            kf = k.astype(F32)
            qt = (qf * x[0:C]).astype(BF16)
            kt = (kf * x[C:2 * C]).astype(BF16)
            a = lax.dot_general(q, k, nt, preferred_element_type=F32)
            a = jnp.where(lev == -1, a, 0.0)
            for j, m in enumerate(REC_LEVELS):
                q_side = ((row & m) != 0) if direction == 0 else ((row & m) == 0)
                w = (jnp.where(q_side, qf, kf) * x[(2 + j) * C:(3 + j) * C]).astype(BF16)
                al = lax.dot_general(w, w, nt, preferred_element_type=F32)
                a = jnp.where(lev == (m.bit_length() - 1), al, a)
            o = jnp.dot(a.astype(BF16), v, preferred_element_type=F32)
            o = o + lax.dot_general(qt, st.astype(BF16), nt, preferred_element_type=F32)
            upd = lax.dot_general(v, kt, tn, preferred_element_type=F32)
            st = st * x[last:last + 1] + upd
            if direction == 0:
                acc_ref[rows, :] = o
            else:
                acc_ref[rows, :] += o
            return st

        lax.fori_loop(0, n_chunks, chunk, jnp.zeros((REC_DV, REC_DK), F32))

    run(0, kf_ref, lf_ref)
    run(1, kb_ref, lb_ref)

    ng = ng_ref[...]
    RB = 256

    def finish(i, carry):
        rows = pl.ds(pl.multiple_of(i * RB, RB), RB)
        o = acc_ref[rows, :]
        o = o * lax.rsqrt(jnp.mean(o * o, axis=-1, keepdims=True) + RMS_EPS) * ng
        o_ref[0, rows, :] = (o * g_ref[0, rows, :].astype(F32)).astype(BF16)
        return carry

    lax.fori_loop(0, S // RB, finish, 0)


def _rec(q, kf, kb, lf, lb, v, g, norm_g):
    B, S, W = q.shape
    m3, lev = _rec_constants()
    head = lambda b, h: (b, 0, h)
    blk = pl.BlockSpec((1, S, LANES), head)
    return pl.pallas_call(
        _rec_kernel,
        grid=(B, REC_HEADS),
        in_specs=[blk, blk, blk, blk, blk, blk, blk,
                  pl.BlockSpec((1, REC_DV), lambda b, h: (0, 0)),
                  pl.BlockSpec(m3.shape, lambda b, h: (0, 0, 0)),
                  pl.BlockSpec(lev.shape, lambda b, h: (0, 0, 0))],
        out_specs=blk,
        out_shape=jax.ShapeDtypeStruct((B, S, W), BF16),
        scratch_shapes=[pltpu.VMEM((S, REC_DV), F32)],
        compiler_params=pltpu.CompilerParams(vmem_limit_bytes=VMEM_LIMIT),
        name="rec",
    )(q, kf, kb, lf, lb, v, g, norm_g.reshape(1, REC_DV), m3, lev)


def _att_bias(S):
    rowi = np.arange(ATT_BLOCK)[:, None]
    coli = np.arange(3 * ATT_BLOCK)[None, :]
    out = []
    for off in (0, ATT_BLOCK, 2 * ATT_BLOCK):
        ok = np.abs(coli - off - rowi) <= ATT_WINDOW
        out.append(np.where(ok, 0.0, NEG_BIG))
    return jnp.asarray(np.stack(out), F32)


def _mix_kernel(x_ref, ga_ref, sc_ref, sh_ref, aq_ref, ak_ref, av_ref, orec_ref, gr_ref, gt_ref,
                sink_ref, bias_ref, wrec_ref, watt_ref, wout_ref, lg_ref, lb_ref,
                x1_o, u2_o, kvar, vvar, att_scr):
    t = pl.program_id(1)
    S = ak_ref.shape[1]
    TQ = x_ref.shape[1]
    nb = S // ATT_BLOCK
    KW = 3 * ATT_BLOCK
    HALF = LANES // 2
    nt = (((1,), (1,)), ((), ()))

    @pl.when(t == 0)
    def _():
        lane = lax.broadcasted_iota(jnp.int32, (S, LANES), 1)
        lo = lane < HALF
        k = ak_ref[0].astype(F32)
        v = av_ref[0].astype(F32)
        ksw = pltpu.roll(k, HALF, axis=1)
        vsw = pltpu.roll(v, HALF, axis=1)
        zero = jnp.zeros_like(k)
        for i, (kk, vv) in enumerate([(jnp.where(lo, k, zero), jnp.where(lo, v, zero)),
                                      (jnp.where(lo, zero, ksw), jnp.where(lo, zero, vsw)),
                                      (jnp.where(lo, ksw, zero), jnp.where(lo, vsw, zero)),
                                      (jnp.where(lo, zero, k), jnp.where(lo, zero, v))]):
            kvar[i] = kk.astype(BF16)
            vvar[i, :, 0:LANES] = vv.astype(BF16)
            ones = jnp.where(lo, 1.0, 0.0) if i % 2 == 0 else jnp.where(lo, 0.0, 1.0)
            vvar[i, :, LANES:2 * LANES] = ones.astype(BF16)

    lane_q = lax.broadcasted_iota(jnp.int32, (ATT_BLOCK, LANES), 1)
    lo_q = lane_q < HALF
    for jj in range(TQ // ATT_BLOCK):
        j = t * (TQ // ATT_BLOCK) + jj
        ks = pl.multiple_of(jnp.clip((j - 1) * ATT_BLOCK, 0, S - KW), ATT_BLOCK)
        bidx = jnp.where(j == 0, 0, jnp.where(j == nb - 1, 2, 1))
        bias = bias_ref[bidx]
        win = pl.ds(ks, KW)
        for s in range(ATT_QWIDTH // LANES):
            hk = (2 * s) // (ATT_Q_HEADS // ATT_KV_HEADS)
            qs = aq_ref[0, jj * ATT_BLOCK:(jj + 1) * ATT_BLOCK, s * LANES:(s + 1) * LANES]
            acc = jnp.zeros((ATT_BLOCK, 2 * LANES), F32)
            den_sink = []
            for par in range(2):
                sc = lax.dot_general(qs, kvar[2 * hk + par, win, :], nt,
                                     preferred_element_type=F32) + bias
                sk = sink_ref[0, 2 * s + par]
                mx = jnp.maximum(jnp.max(sc, axis=-1, keepdims=True), sk)
                p = jnp.exp(sc - mx).astype(BF16)
                acc = acc + jnp.dot(p, vvar[2 * hk + par, win, :], preferred_element_type=F32)
                den_sink.append(jnp.exp(sk - mx))
            den = acc[:, LANES:] + jnp.where(lo_q, den_sink[0], den_sink[1])
            att_scr[jj * ATT_BLOCK:(jj + 1) * ATT_BLOCK, s * LANES:(s + 1) * LANES] = (
                acc[:, :LANES] / den).astype(BF16)

    y_att = jnp.dot(att_scr[...], watt_ref[...], preferred_element_type=F32)
    y_rec = jnp.dot(orec_ref[0], wrec_ref[...], preferred_element_type=F32)
    merged = gr_ref[0].astype(F32) * y_rec + gt_ref[0].astype(F32) * y_att
    z = jnp.dot(merged.astype(BF16), wout_ref[...], preferred_element_type=F32)
    r = DEEPNORM_ALPHA * x_ref[0] + (1.0 + ga_ref[0]) * z
    x1 = _layer_norm(r, lg_ref[...], lb_ref[...])
    x1_o[0] = x1
    u2_o[0] = (x1 * (1.0 + sc_ref[0]) + sh_ref[0]).astype(BF16)


def _mix(x, ga1, sc2, sh2, aq, ak, av, orec, g_rec, g_att, sink, w_rec, w_att, w_out, ln_g, ln_b, tq):
    B, S, D = x.shape
    bias = _att_bias(S)
    tok = lambda b, t: (b, t, 0)
    per_b = lambda b, t: (b, 0, 0)
    c2 = lambda b, t: (0, 0)
    full = lambda a: pl.BlockSpec(a.shape, (lambda b, t: (0,) * a.ndim))
    return pl.pallas_call(
        _mix_kernel,
        grid=(B, S // tq),
        in_specs=[pl.BlockSpec((1, tq, D), tok),
                  pl.BlockSpec((1, 1, D), per_b), pl.BlockSpec((1, 1, D), per_b),
                  pl.BlockSpec((1, 1, D), per_b),
                  pl.BlockSpec((1, tq, ATT_QWIDTH), tok),
                  pl.BlockSpec((1, S, ATT_KVWIDTH), per_b),
                  pl.BlockSpec((1, S, ATT_KVWIDTH), per_b),
                  pl.BlockSpec((1, tq, REC_WIDTH), tok),
                  pl.BlockSpec((1, tq, D), tok), pl.BlockSpec((1, tq, D), tok),
                  pl.BlockSpec(memory_space=pltpu.SMEM),
                  full(bias), full(w_rec), full(w_att), full(w_out),
                  pl.BlockSpec((1, D), c2), pl.BlockSpec((1, D), c2)],
        out_specs=[pl.BlockSpec((1, tq, D), tok), pl.BlockSpec((1, tq, D), tok)],
        out_shape=[jax.ShapeDtypeStruct((B, S, D), F32), jax.ShapeDtypeStruct((B, S, D), BF16)],
        scratch_shapes=[pltpu.VMEM((4, S, LANES), BF16),
                        pltpu.VMEM((4, S, 2 * LANES), BF16),
                        pltpu.VMEM((tq, ATT_QWIDTH), BF16)],
        compiler_params=pltpu.CompilerParams(
            dimension_semantics=("arbitrary", "arbitrary"), vmem_limit_bytes=VMEM_LIMIT),
        name="mix",
    )(x, ga1, sc2, sh2, aq, ak, av, orec, g_rec, g_att, sink.reshape(1, ATT_Q_HEADS), bias,
      w_rec, w_att, w_out, ln_g.reshape(1, D), ln_b.reshape(1, D))


HALO = 16
FFN_CHUNK = 256


def _gelu_tanh(x):
    return 0.5 * x * (1.0 + jnp.tanh(np.sqrt(2.0 / np.pi).astype(np.float32) * (x + 0.044715 * (x * x * x))))


def _ffn_kernel(u_ref, up_ref, un_ref, x1_ref, ga_ref, wup_ref, cw_ref, cb_ref, wdn_ref,
                lg_ref, lb_ref, o_ref, ucat):
    t = pl.program_id(1)
    nt_ = pl.num_programs(1)
    tm = u_ref.shape[1]
    d_ff = wdn_ref.shape[0]
    ucat[0:HALO] = jnp.where(t > 0, up_ref[0], jnp.zeros_like(up_ref[0]))
    ucat[HALO:HALO + tm] = u_ref[0]
    ucat[HALO + tm:] = jnp.where(t < nt_ - 1, un_ref[0], jnp.zeros_like(un_ref[0]))
    u = ucat[...]

    def conv(col):
        h = jnp.dot(u, wup_ref[:, col:col + FFN_CHUNK], preferred_element_type=F32)
        w = cw_ref[:, col:col + FFN_CHUNK]
        out = cb_ref[:, col:col + FFN_CHUNK]
        for tap in range(CONV_K):
            lo = HALO + tap - CONV_K // 2
            out = out + h[lo:lo + tm] * w[tap:tap + 1]
        return out

    acc = jnp.zeros((tm, o_ref.shape[-1]), F32)
    for j in range(d_ff // FFN_CHUNK):
        val = conv(j * FFN_CHUNK)
        gate = conv(d_ff + j * FFN_CHUNK)
        a = (_gelu_tanh(gate) * val).astype(BF16)
        acc = acc + jnp.dot(a, wdn_ref[j * FFN_CHUNK:(j + 1) * FFN_CHUNK, :],
                            preferred_element_type=F32)
    r = DEEPNORM_ALPHA * x1_ref[0] + (1.0 + ga_ref[0]) * acc
    o_ref[0] = _layer_norm(r, lg_ref[...], lb_ref[...])


def _ffn(u2, x1, ga2, w_up, conv_w, conv_b, w_down, ln_g, ln_b, tm):
    B, S, D = x1.shape
    d_ff = w_down.shape[0]
    assert d_ff % FFN_CHUNK == 0 and tm % HALO == 0
    nh = tm // HALO
    last = S // HALO - 1
    tok = lambda b, t: (b, t, 0)
    per_b = lambda b, t: (b, 0, 0)
    c2 = lambda b, t: (0, 0)
    return pl.pallas_call(
        _ffn_kernel,
        grid=(B, S // tm),
        in_specs=[pl.BlockSpec((1, tm, D), tok),
                  pl.BlockSpec((1, HALO, D), lambda b, t: (b, jnp.maximum(t * nh - 1, 0), 0)),
                  pl.BlockSpec((1, HALO, D), lambda b, t: (b, jnp.minimum((t + 1) * nh, last), 0)),
                  pl.BlockSpec((1, tm, D), tok),
                  pl.BlockSpec((1, 1, D), per_b),
                  pl.BlockSpec(w_up.shape, c2, pipeline_mode=pl.Buffered(1)),
                  pl.BlockSpec(conv_w.shape, c2),
                  pl.BlockSpec((1, 2 * d_ff), c2),
                  pl.BlockSpec(w_down.shape, c2, pipeline_mode=pl.Buffered(1)),
                  pl.BlockSpec((1, D), c2), pl.BlockSpec((1, D), c2)],
        out_specs=pl.BlockSpec((1, tm, D), tok),
        out_shape=jax.ShapeDtypeStruct((B, S, D), F32),
        scratch_shapes=[pltpu.VMEM((tm + 2 * HALO, D), BF16)],
        compiler_params=pltpu.CompilerParams(vmem_limit_bytes=VMEM_LIMIT),
        name="ffn",
    )(u2, u2, u2, x1, ga2, w_up, conv_w, conv_b.reshape(1, 2 * d_ff), w_down,
      ln_g.reshape(1, D), ln_b.reshape(1, D))


def kernel(x, c, positions, w_ada, b_ada, w_in, rec_lower_bound, rec_norm_g, w_rec_branch,
           attn_sink, w_attn_branch, w_out, ln1_g, ln1_b, w_up, conv_w, conv_b, w_down,
           ln2_g, ln2_b):
    B, S, D = x.shape
    assert w_ada.shape[0] == DEPTH and rec_lower_bound.shape[1] == DEPTH + 1
    tm_proj = min(512, S)
    tq = min(256, S)
    tm_ffn = min(512, S)
    for l in range(DEPTH):
        mods = _ada(c, w_ada[l], b_ada[l])
        sh1, sc1, ga1, sh2, sc2, ga2 = [m.reshape(B, 1, D) for m in jnp.split(mods, 6, axis=-1)]
        (q, kf, kb, lf, lb, v, g, aq, ak, av, g_rec, g_att) = _proj(
            x, sc1, sh1, positions, w_in[l].astype(BF16), rec_lower_bound, tm_proj)
        orec = _rec(q, kf, kb, lf, lb, v, g, rec_norm_g[l])
        x1, u2 = _mix(x, ga1, sc2, sh2, aq, ak, av, orec, g_rec, g_att, attn_sink[l],
                      w_rec_branch[l].astype(BF16), w_attn_branch[l].astype(BF16),
                      w_out[l].astype(BF16), ln1_g[l], ln1_b[l], tq)
        x = _ffn(u2, x1, ga2, w_up[l].astype(BF16), conv_w[l], conv_b[l], w_down[l].astype(BF16),
                 ln2_g[l], ln2_b[l], tm_ffn)
    return x
```

```python
import functools

import numpy as np
import jax
import jax.numpy as jnp
from jax import lax
from jax.experimental import pallas as pl
from jax.experimental.pallas import tpu as pltpu

F32 = jnp.float32
BF16 = jnp.bfloat16

REC_HEADS = 4
REC_DK = 128
REC_DV = 128
REC_WIDTH = REC_HEADS * REC_DK
ATT_Q_HEADS = 8
ATT_KV_HEADS = 2
ATT_HEAD_DIM = 64
ATT_QWIDTH = ATT_Q_HEADS * ATT_HEAD_DIM
ATT_KVWIDTH = ATT_KV_HEADS * ATT_HEAD_DIM
ATT_WINDOW = 128
ATT_BLOCK = 128
ROPE_THETA = 10000.0
CONV_K = 3
LN_EPS = 1e-5
RMS_EPS = 1e-6
DEPTH = 1
DEEPNORM_ALPHA = (2 * DEPTH) ** 0.25

LANES = 128
REC_CHUNK = 64
REC_LEVELS = (32, 16, 8, 4, 2, 1)
NEG_BIG = -1e30
VMEM_LIMIT = 56 * 1024 * 1024


def _sigmoid(x):
    return 1.0 / (1.0 + jnp.exp(-x))


def _silu(x):
    return x * _sigmoid(x)


def _layer_norm(r, g, b):
    mu = jnp.mean(r, axis=-1, keepdims=True)
    rc = r - mu
    var = jnp.mean(rc * rc, axis=-1, keepdims=True)
    return rc * lax.rsqrt(var + LN_EPS) * g + b


def _ada_kernel(c_ref, w_ref, b_ref, o_ref):
    c = c_ref[...]
    o_ref[...] = jnp.dot(_silu(c), w_ref[...], preferred_element_type=F32) + b_ref[...]


def _ada(c, w_ada, b_ada):
    B, D = c.shape
    N = w_ada.shape[1]
    return pl.pallas_call(
        _ada_kernel,
        grid=(N // D,),
        in_specs=[pl.BlockSpec((B, D), lambda j: (0, 0)),
                  pl.BlockSpec((D, D), lambda j: (0, j)),
                  pl.BlockSpec((1, D), lambda j: (0, j))],
        out_specs=pl.BlockSpec((B, D), lambda j: (0, j)),
        out_shape=jax.ShapeDtypeStruct((B, N), F32),
        name="ada",
    )(c, w_ada, b_ada.reshape(1, N))


def _proj_kernel(x_ref, sc_ref, sh_ref, pos_ref, invf_ref, w_ref, rlb_ref,
                 q_o, kf_o, kb_o, lf_o, lb_o, v_o, g_o, aq_o, ak_o, av_o, gr_o, ga_o):
    u = (x_ref[0] * (1.0 + sc_ref[0]) + sh_ref[0]).astype(BF16)

    def mm(lo, width):
        return jnp.dot(u, w_ref[:, lo:lo + width], preferred_element_type=F32)

    W = REC_WIDTH
    q_o[0] = (_silu(mm(0, W)) * (REC_DK ** -0.5)).astype(BF16)

    def forget(col, direction, k_o, l_o):
        r = rlb_ref[direction]
        e = jnp.exp(r - jnp.max(r, axis=0, keepdims=True))
        lb = e[0:1] / jnp.sum(e, axis=0, keepdims=True)
        f = lb + (1.0 - lb) * _sigmoid(mm(col, W))
        k_o[0] = (1.0 - f).astype(BF16)
        l_o[0] = jnp.log(f)

    forget(W, 0, kf_o, lf_o)
    forget(2 * W, 1, kb_o, lb_o)
    v_o[0] = mm(3 * W, W).astype(BF16)
    g_o[0] = _silu(mm(4 * W, W)).astype(BF16)

    ang = pos_ref[0].astype(F32) * invf_ref[...]
    cos = jnp.cos(ang)
    sin = jnp.sin(ang)
    lane = lax.broadcasted_iota(jnp.int32, ang.shape, 1)
    first_half = (lane % ATT_HEAD_DIM) < (ATT_HEAD_DIM // 2)
    sin_signed = jnp.where(first_half, -sin, sin)

    def rope(t):
        partner = jnp.where(first_half,
                            pltpu.roll(t, LANES - ATT_HEAD_DIM // 2, axis=1),
                            pltpu.roll(t, ATT_HEAD_DIM // 2, axis=1))
        return t * cos + partner * sin_signed

    c0 = 5 * W
    aq = mm(c0, ATT_QWIDTH)
    for s in range(ATT_QWIDTH // LANES):
        sl = slice(s * LANES, (s + 1) * LANES)
        aq_o[0, :, sl] = (rope(aq[:, sl]) * (ATT_HEAD_DIM ** -0.5)).astype(BF16)
    c0 += ATT_QWIDTH
    ak_o[0] = rope(mm(c0, ATT_KVWIDTH)).astype(BF16)
    c0 += ATT_KVWIDTH
    av_o[0] = mm(c0, ATT_KVWIDTH).astype(BF16)
    c0 += ATT_KVWIDTH
    D = x_ref.shape[-1]
    gr_o[0] = _sigmoid(mm(c0, D)).astype(BF16)
    ga_o[0] = _sigmoid(mm(c0 + D, D)).astype(BF16)


def _proj(x, sc1, sh1, positions, w_in_bf, rec_lower_bound, tm):
    B, S, D = x.shape
    d_in = w_in_bf.shape[1]
    half = ATT_HEAD_DIM // 2
    inv_freq = ROPE_THETA ** (-jnp.arange(half, dtype=F32) / half)
    invf = jnp.tile(inv_freq, LANES // half).reshape(1, LANES)
    W = REC_WIDTH
    tok = lambda b, t: (b, t, 0)
    per_b = lambda b, t: (b, 0, 0)
    outs = [(W, BF16), (W, BF16), (W, BF16), (W, F32), (W, F32), (W, BF16), (W, BF16),
            (ATT_QWIDTH, BF16), (ATT_KVWIDTH, BF16), (ATT_KVWIDTH, BF16), (D, BF16), (D, BF16)]
    return pl.pallas_call(
        _proj_kernel,
        grid=(B, S // tm),
        in_specs=[pl.BlockSpec((1, tm, D), tok),
                  pl.BlockSpec((1, 1, D), per_b),
                  pl.BlockSpec((1, 1, D), per_b),
                  pl.BlockSpec((1, tm, 1), tok),
                  pl.BlockSpec((1, LANES), lambda b, t: (0, 0)),
                  pl.BlockSpec((D, d_in), lambda b, t: (0, 0), pipeline_mode=pl.Buffered(1)),
                  pl.BlockSpec(rec_lower_bound.shape, lambda b, t: (0, 0, 0))],
        out_specs=[pl.BlockSpec((1, tm, w), tok) for w, _ in outs],
        out_shape=[jax.ShapeDtypeStruct((B, S, w), dt) for w, dt in outs],
        compiler_params=pltpu.CompilerParams(vmem_limit_bytes=VMEM_LIMIT),
        name="proj",
    )(x, sc1, sh1, positions.reshape(B, S, 1), invf, w_in_bf, rec_lower_bound)


def _rec_constants():
    C = REC_CHUNK
    nblk = 2 + len(REC_LEVELS)
    mf = np.zeros((nblk, C, C), np.float32)
    r = np.arange(C)
    for t in range(C):
        mf[0, t] = r <= t
        mf[1, t] = r > t
        for i, m in enumerate(REC_LEVELS):
            boundary = (t // (2 * m)) * 2 * m + m
            if t >= boundary:
                mf[2 + i, t] = (r >= boundary) & (r <= t)
            else:
                mf[2 + i, t] = (r > t) & (r <= boundary - 1)
    mb = mf[:, ::-1, ::-1]
    x = r[:, None] ^ r[None, :]
    lev = np.where(x > 0, np.floor(np.log2(np.maximum(x, 1))).astype(np.int32), -1)
    invalid = -2
    levf = np.where(r[:, None] >= r[None, :], lev, invalid).astype(np.int32)
    levb = np.where(r[:, None] <= r[None, :], lev, invalid).astype(np.int32)
    m3 = lambda m: np.concatenate([m.reshape(nblk * C, C)] * 3, axis=1)
    return (jnp.asarray(np.stack([m3(mf), m3(mb)]), BF16),
            jnp.asarray(np.stack([levf, levb])))


REC_HEADS_PER_STEP = 2


def _rec_kernel(q_ref, kf_ref, kb_ref, lf_ref, lb_ref, v_ref, g_ref, ng_ref, m_ref, lev_ref,
                o_ref, acc_ref, qt_ref, upd_ref, dec_ref):
    C = REC_CHUNK
    HP = REC_HEADS_PER_STEP
    S = q_ref.shape[1]
    n_chunks = S // C
    row = lax.broadcasted_iota(jnp.int32, (C, LANES), 0)
    nt = (((1,), (1,)), ((), ()))
    tn = (((0,), (0,)), ((), ()))
    k_refs = (kf_ref, kb_ref)
    l_refs = (lf_ref, lb_ref)

    def local(i, carry):
        ns = (i, n_chunks - 1 - i)
        rows = [pl.ds(pl.multiple_of(n * C, C), C) for n in ns]
        units = [(d, h) for d in range(2) for h in range(HP)]
        sls = [slice(h * LANES, (h + 1) * LANES) for h in range(HP)]
        xs = []
        for d in range(2):
            lg = l_refs[d][0, rows[d], :]
            hi = lg.astype(BF16)
            r1 = lg - hi.astype(F32)
            mid = r1.astype(BF16)
            lo = (r1 - mid.astype(F32)).astype(BF16)
            e = jnp.dot(m_ref[d], jnp.concatenate([hi, mid, lo], axis=0),
                        preferred_element_type=F32)
            xs.append(jnp.exp(e))
        ws, kts, qs, ks, vs = {}, {}, {}, {}, {}
        for d in range(2):
            last = C - 1 if d == 0 else 0
            dec_ref[d, pl.ds(ns[d], 1), :] = xs[d][last:last + 1]
            for h in range(HP):
                q = q_ref[0, rows[d], sls[h]]
                k = k_refs[d][0, rows[d], sls[h]]
                qs[d, h], ks[d, h], vs[d, h] = q, k, v_ref[0, rows[d], sls[h]]
                qf = q.astype(F32)
                kf = k.astype(F32)
                qt_ref[d, rows[d], sls[h]] = (qf * xs[d][0:C, sls[h]]).astype(BF16)
                kts[d, h] = (kf * xs[d][C:2 * C, sls[h]]).astype(BF16)
                for j, m in enumerate(REC_LEVELS):
                    q_side = ((row & m) != 0) if d == 0 else ((row & m) == 0)
                    ws[d, h, j] = (jnp.where(q_side, qf, kf)
                                   * xs[d][(2 + j) * C:(3 + j) * C, sls[h]]).astype(BF16)
        als = {}
        for d, h in units:
            als[d, h, -1] = lax.dot_general(qs[d, h], ks[d, h], nt, preferred_element_type=F32)
            for j in range(len(REC_LEVELS)):
                als[d, h, j] = lax.dot_general(ws[d, h, j], ws[d, h, j], nt,
                                               preferred_element_type=F32)
        for d, h in units:
            upd_ref[d, h, ns[d]] = lax.dot_general(vs[d, h], kts[d, h], tn,
                                                   preferred_element_type=F32)
        avs = {}
        for d, h in units:
            lev = lev_ref[d]
            a = jnp.where(lev == -1, als[d, h, -1], 0.0)
            for j, m in enumerate(REC_LEVELS):
                a = jnp.where(lev == (m.bit_length() - 1), als[d, h, j], a)
            avs[d, h] = a.astype(BF16)
        for d, h in units:
            acc_ref[d, rows[d], sls[h]] = jnp.dot(avs[d, h], vs[d, h], preferred_element_type=F32)
        return carry

    lax.fori_loop(0, n_chunks, local, 0)

    for direction in range(2):
        def carried(i, sts, direction=direction):
            n = i if direction == 0 else n_chunks - 1 - i
            rows = pl.ds(pl.multiple_of(n * C, C), C)
            d = dec_ref[direction, pl.ds(n, 1), :]
            out = []
            for h in range(HP):
                sl = slice(h * LANES, (h + 1) * LANES)
                st = sts[h]
                acc_ref[direction, rows, sl] += lax.dot_general(
                    qt_ref[direction, rows, sl], st.astype(BF16), nt, preferred_element_type=F32)
                out.append(st * d[:, sl] + upd_ref[direction, h, n])
            return tuple(out)

        lax.fori_loop(0, n_chunks, carried,
                      tuple(jnp.zeros((REC_DV, REC_DK), F32) for _ in range(HP)), unroll=4)

    RB = 256

    def finish(i, carry):
        rows = pl.ds(pl.multiple_of(i * RB, RB), RB)
        for h in range(HP):
            sl = slice(h * LANES, (h + 1) * LANES)
            o = acc_ref[0, rows, sl] + acc_ref[1, rows, sl]
            o = o * lax.rsqrt(jnp.mean(o * o, axis=-1, keepdims=True) + RMS_EPS) * ng_ref[...]
            o_ref[0, rows, sl] = (o * g_ref[0, rows, sl].astype(F32)).astype(BF16)
        return carry

    lax.fori_loop(0, S // RB, finish, 0)


def _rec(q, kf, kb, lf, lb, v, g, norm_g):
    B, S, W = q.shape
    HP = REC_HEADS_PER_STEP
    m3, lev = _rec_constants()
    n_chunks = S // REC_CHUNK
    head = lambda b, h: (b, 0, h)
    blk = pl.BlockSpec((1, S, HP * LANES), head)
    return pl.pallas_call(
        _rec_kernel,
        grid=(B, REC_HEADS // HP),
        in_specs=[blk, blk, blk, blk, blk, blk, blk,
                  pl.BlockSpec((1, REC_DV), lambda b, h: (0, 0)),
                  pl.BlockSpec(m3.shape, lambda b, h: (0, 0, 0)),
                  pl.BlockSpec(lev.shape, lambda b, h: (0, 0, 0))],
        out_specs=blk,
        out_shape=jax.ShapeDtypeStruct((B, S, W), BF16),
        scratch_shapes=[pltpu.VMEM((2, S, HP * REC_DV), F32),
                        pltpu.VMEM((2, S, HP * REC_DK), BF16),
                        pltpu.VMEM((2, HP, n_chunks, REC_DV, REC_DK), F32),
                        pltpu.VMEM((2, n_chunks, HP * REC_DK), F32)],
        compiler_params=pltpu.CompilerParams(vmem_limit_bytes=VMEM_LIMIT),
        name="rec",
    )(q, kf, kb, lf, lb, v, g, norm_g.reshape(1, REC_DV), m3, lev)


def _att_bias(S):
    rowi = np.arange(ATT_BLOCK)[:, None]
    coli = np.arange(3 * ATT_BLOCK)[None, :]
    out = []
    for off in (0, ATT_BLOCK, 2 * ATT_BLOCK):
        ok = np.abs(coli - off - rowi) <= ATT_WINDOW
        out.append(np.where(ok, 0.0, NEG_BIG))
    return jnp.asarray(np.stack(out), F32)


def _mix_kernel(x_ref, ga_ref, sc_ref, sh_ref, aq_ref, ak_ref, av_ref, orec_ref, gr_ref, gt_ref,
                sink_ref, bias_ref, wrec_ref, watt_ref, wout_ref, lg_ref, lb_ref,
                x1_o, u2_o, kvar, vvar, att_scr):
    t = pl.program_id(1)
    S = ak_ref.shape[1]
    TQ = x_ref.shape[1]
    nb = S // ATT_BLOCK
    KW = 3 * ATT_BLOCK
    HALF = LANES // 2
    nt = (((1,), (1,)), ((), ()))

    @pl.when(t == 0)
    def _():
        lane = lax.broadcasted_iota(jnp.int32, (S, LANES), 1)
        lo = lane < HALF
        k = ak_ref[0].astype(F32)
        v = av_ref[0].astype(F32)
        ksw = pltpu.roll(k, HALF, axis=1)
        vsw = pltpu.roll(v, HALF, axis=1)
        zero = jnp.zeros_like(k)
        for i, (kk, vv) in enumerate([(jnp.where(lo, k, zero), jnp.where(lo, v, zero)),
                                      (jnp.where(lo, zero, ksw), jnp.where(lo, zero, vsw)),
                                      (jnp.where(lo, ksw, zero), jnp.where(lo, vsw, zero)),
                                      (jnp.where(lo, zero, k), jnp.where(lo, zero, v))]):
            kvar[i] = kk.astype(BF16)
            vvar[i, :, 0:LANES] = vv.astype(BF16)
            ones = jnp.where(lo, 1.0, 0.0) if i % 2 == 0 else jnp.where(lo, 0.0, 1.0)
            vvar[i, :, LANES:2 * LANES] = ones.astype(BF16)

    lane_q = lax.broadcasted_iota(jnp.int32, (ATT_BLOCK, LANES), 1)
    lo_q = lane_q < HALF
    for jj in range(TQ // ATT_BLOCK):
        j = t * (TQ // ATT_BLOCK) + jj
        ks = pl.multiple_of(jnp.clip((j - 1) * ATT_BLOCK, 0, S - KW), ATT_BLOCK)
        bidx = jnp.where(j == 0, 0, jnp.where(j == nb - 1, 2, 1))
        bias = bias_ref[bidx]
        win = pl.ds(ks, KW)
        for s in range(ATT_QWIDTH // LANES):
            hk = (2 * s) // (ATT_Q_HEADS // ATT_KV_HEADS)
            qs = aq_ref[0, jj * ATT_BLOCK:(jj + 1) * ATT_BLOCK, s * LANES:(s + 1) * LANES]
            acc = jnp.zeros((ATT_BLOCK, 2 * LANES), F32)
            den_sink = []
            for par in range(2):
                sc = lax.dot_general(qs, kvar[2 * hk + par, win, :], nt,
                                     preferred_element_type=F32) + bias
                sk = sink_ref[0, 2 * s + par]
                mx = jnp.maximum(jnp.max(sc, axis=-1, keepdims=True), sk)
                p = jnp.exp(sc - mx).astype(BF16)
                acc = acc + jnp.dot(p, vvar[2 * hk + par, win, :], preferred_element_type=F32)
                den_sink.append(jnp.exp(sk - mx))
            den = acc[:, LANES:] + jnp.where(lo_q, den_sink[0], den_sink[1])
            att_scr[jj * ATT_BLOCK:(jj + 1) * ATT_BLOCK, s * LANES:(s + 1) * LANES] = (
                acc[:, :LANES] / den).astype(BF16)

    y_att = jnp.dot(att_scr[...], watt_ref[...], preferred_element_type=F32)
    y_rec = jnp.dot(orec_ref[0], wrec_ref[...], preferred_element_type=F32)
    merged = gr_ref[0].astype(F32) * y_rec + gt_ref[0].astype(F32) * y_att
    z = jnp.dot(merged.astype(BF16), wout_ref[...], preferred_element_type=F32)
    r = DEEPNORM_ALPHA * x_ref[0] + (1.0 + ga_ref[0]) * z
    x1 = _layer_norm(r, lg_ref[...], lb_ref[...])
    x1_o[0] = x1
    u2_o[0] = (x1 * (1.0 + sc_ref[0]) + sh_ref[0]).astype(BF16)


def _mix(x, ga1, sc2, sh2, aq, ak, av, orec, g_rec, g_att, sink, w_rec, w_att, w_out, ln_g, ln_b, tq):
    B, S, D = x.shape
    bias = _att_bias(S)
    tok = lambda b, t: (b, t, 0)
    per_b = lambda b, t: (b, 0, 0)
    c2 = lambda b, t: (0, 0)
    full = lambda a: pl.BlockSpec(a.shape, (lambda b, t: (0,) * a.ndim))
    return pl.pallas_call(
        _mix_kernel,
        grid=(B, S // tq),
        in_specs=[pl.BlockSpec((1, tq, D), tok),
                  pl.BlockSpec((1, 1, D), per_b), pl.BlockSpec((1, 1, D), per_b),
                  pl.BlockSpec((1, 1, D), per_b),
                  pl.BlockSpec((1, tq, ATT_QWIDTH), tok),
                  pl.BlockSpec((1, S, ATT_KVWIDTH), per_b),
                  pl.BlockSpec((1, S, ATT_KVWIDTH), per_b),
                  pl.BlockSpec((1, tq, REC_WIDTH), tok),
                  pl.BlockSpec((1, tq, D), tok), pl.BlockSpec((1, tq, D), tok),
                  pl.BlockSpec(memory_space=pltpu.SMEM),
                  full(bias), full(w_rec), full(w_att), full(w_out),
                  pl.BlockSpec((1, D), c2), pl.BlockSpec((1, D), c2)],
        out_specs=[pl.BlockSpec((1, tq, D), tok), pl.BlockSpec((1, tq, D), tok)],
        out_shape=[jax.ShapeDtypeStruct((B, S, D), F32), jax.ShapeDtypeStruct((B, S, D), BF16)],
        scratch_shapes=[pltpu.VMEM((4, S, LANES), BF16),
                        pltpu.VMEM((4, S, 2 * LANES), BF16),
                        pltpu.VMEM((tq, ATT_QWIDTH), BF16)],
        compiler_params=pltpu.CompilerParams(
            dimension_semantics=("arbitrary", "arbitrary"), vmem_limit_bytes=VMEM_LIMIT),
        name="mix",
    )(x, ga1, sc2, sh2, aq, ak, av, orec, g_rec, g_att, sink.reshape(1, ATT_Q_HEADS), bias,
      w_rec, w_att, w_out, ln_g.reshape(1, D), ln_b.reshape(1, D))


HALO = 16
FFN_CHUNK = 256


def _gelu_tanh(x):
    return 0.5 * x * (1.0 + jnp.tanh(np.sqrt(2.0 / np.pi).astype(np.float32) * (x + 0.044715 * (x * x * x))))


def _ffn_kernel(u_ref, up_ref, un_ref, x1_ref, ga_ref, wup_ref, cw_ref, cb_ref, wdn_ref,
                lg_ref, lb_ref, o_ref, ucat):
    t = pl.program_id(1)
    nt_ = pl.num_programs(1)
    tm = u_ref.shape[1]
    d_ff = wdn_ref.shape[0]
    ucat[0:HALO] = jnp.where(t > 0, up_ref[0], jnp.zeros_like(up_ref[0]))
    ucat[HALO:HALO + tm] = u_ref[0]
    ucat[HALO + tm:] = jnp.where(t < nt_ - 1, un_ref[0], jnp.zeros_like(un_ref[0]))
    u = ucat[...]

    def conv(col):
        h = jnp.dot(u, wup_ref[:, col:col + FFN_CHUNK], preferred_element_type=F32)
        w = cw_ref[:, col:col + FFN_CHUNK]
        out = cb_ref[:, col:col + FFN_CHUNK]
        for tap in range(CONV_K):
            lo = HALO + tap - CONV_K // 2
            out = out + h[lo:lo + tm] * w[tap:tap + 1]
        return out

    acc = jnp.zeros((tm, o_ref.shape[-1]), F32)
    for j in range(d_ff // FFN_CHUNK):
        val = conv(j * FFN_CHUNK)
        gate = conv(d_ff + j * FFN_CHUNK)
        a = (_gelu_tanh(gate) * val).astype(BF16)
        acc = acc + jnp.dot(a, wdn_ref[j * FFN_CHUNK:(j + 1) * FFN_CHUNK, :],
                            preferred_element_type=F32)
    r = DEEPNORM_ALPHA * x1_ref[0] + (1.0 + ga_ref[0]) * acc
    o_ref[0] = _layer_norm(r, lg_ref[...], lb_ref[...])


def _ffn(u2, x1, ga2, w_up, conv_w, conv_b, w_down, ln_g, ln_b, tm):
    B, S, D = x1.shape
    d_ff = w_down.shape[0]
    assert d_ff % FFN_CHUNK == 0 and tm % HALO == 0
    nh = tm // HALO
    last = S // HALO - 1
    tok = lambda b, t: (b, t, 0)
    per_b = lambda b, t: (b, 0, 0)
    c2 = lambda b, t: (0, 0)
    return pl.pallas_call(
        _ffn_kernel,
        grid=(B, S // tm),
        in_specs=[pl.BlockSpec((1, tm, D), tok),
                  pl.BlockSpec((1, HALO, D), lambda b, t: (b, jnp.maximum(t * nh - 1, 0), 0)),
                  pl.BlockSpec((1, HALO, D), lambda b, t: (b, jnp.minimum((t + 1) * nh, last), 0)),
                  pl.BlockSpec((1, tm, D), tok),
                  pl.BlockSpec((1, 1, D), per_b),
                  pl.BlockSpec(w_up.shape, c2, pipeline_mode=pl.Buffered(1)),
                  pl.BlockSpec(conv_w.shape, c2),
                  pl.BlockSpec((1, 2 * d_ff), c2),
                  pl.BlockSpec(w_down.shape, c2, pipeline_mode=pl.Buffered(1)),
                  pl.BlockSpec((1, D), c2), pl.BlockSpec((1, D), c2)],
        out_specs=pl.BlockSpec((1, tm, D), tok),
        out_shape=jax.ShapeDtypeStruct((B, S, D), F32),
        scratch_shapes=[pltpu.VMEM((tm + 2 * HALO, D), BF16)],
        compiler_params=pltpu.CompilerParams(vmem_limit_bytes=VMEM_LIMIT),
        name="ffn",
    )(u2, u2, u2, x1, ga2, w_up, conv_w, conv_b.reshape(1, 2 * d_ff), w_down,
      ln_g.reshape(1, D), ln_b.reshape(1, D))


def kernel(x, c, positions, w_ada, b_ada, w_in, rec_lower_bound, rec_norm_g, w_rec_branch,
           attn_sink, w_attn_branch, w_out, ln1_g, ln1_b, w_up, conv_w, conv_b, w_down,
           ln2_g, ln2_b):
    B, S, D = x.shape
    assert w_ada.shape[0] == DEPTH and rec_lower_bound.shape[1] == DEPTH + 1
    tm_proj = min(512, S)
    tq = min(256, S)
    tm_ffn = min(512, S)
    for l in range(DEPTH):
        mods = _ada(c, w_ada[l], b_ada[l])
        sh1, sc1, ga1, sh2, sc2, ga2 = [m.reshape(B, 1, D) for m in jnp.split(mods, 6, axis=-1)]
        (q, kf, kb, lf, lb, v, g, aq, ak, av, g_rec, g_att) = _proj(
            x, sc1, sh1, positions, w_in[l].astype(BF16), rec_lower_bound, tm_proj)
        orec = _rec(q, kf, kb, lf, lb, v, g, rec_norm_g[l])
        x1, u2 = _mix(x, ga1, sc2, sh2, aq, ak, av, orec, g_rec, g_att, attn_sink[l],
                      w_rec_branch[l].astype(BF16), w_attn_branch[l].astype(BF16),
                      w_out[l].astype(BF16), ln1_g[l], ln1_b[l], tq)
        x = _ffn(u2, x1, ga2, w_up[l].astype(BF16), conv_w[l], conv_b[l], w_down[l].astype(BF16),
                 ln2_g[l], ln2_b[l], tm_ffn)
    return x
```

```python
import functools

import numpy as np
import jax
import jax.numpy as jnp
from jax import lax
from jax.experimental import pallas as pl
from jax.experimental.pallas import tpu as pltpu

F32 = jnp.float32
BF16 = jnp.bfloat16

REC_HEADS = 4
REC_DK = 128
REC_DV = 128
REC_WIDTH = REC_HEADS * REC_DK
ATT_Q_HEADS = 8
ATT_KV_HEADS = 2
ATT_HEAD_DIM = 64
ATT_QWIDTH = ATT_Q_HEADS * ATT_HEAD_DIM
ATT_KVWIDTH = ATT_KV_HEADS * ATT_HEAD_DIM
ATT_WINDOW = 128
ATT_BLOCK = 128
ROPE_THETA = 10000.0
CONV_K = 3
LN_EPS = 1e-5
RMS_EPS = 1e-6
DEPTH = 1
DEEPNORM_ALPHA = (2 * DEPTH) ** 0.25

LANES = 128
REC_CHUNK = 64
REC_LEVELS = (32, 16, 8, 4, 2, 1)
LOG2E = 1.4426950408889634
NEG_BIG = -1e30
VMEM_LIMIT = 56 * 1024 * 1024


def _sigmoid(x):
    return 1.0 / (1.0 + jnp.exp(-x))


def _silu(x):
    return x * _sigmoid(x)


def _layer_norm(r, g, b):
    mu = jnp.mean(r, axis=-1, keepdims=True)
    rc = r - mu
    var = jnp.mean(rc * rc, axis=-1, keepdims=True)
    return rc * lax.rsqrt(var + LN_EPS) * g + b


def _ada_kernel(c_ref, w_ref, b_ref, o_ref):
    c = c_ref[...]
    o_ref[...] = jnp.dot(_silu(c), w_ref[...], preferred_element_type=F32) + b_ref[...]


def _ada(c, w_ada, b_ada):
    B, D = c.shape
    N = w_ada.shape[1]
    return pl.pallas_call(
        _ada_kernel,
        grid=(N // D,),
        in_specs=[pl.BlockSpec((B, D), lambda j: (0, 0)),
                  pl.BlockSpec((D, D), lambda j: (0, j)),
                  pl.BlockSpec((1, D), lambda j: (0, j))],
        out_specs=pl.BlockSpec((B, D), lambda j: (0, j)),
        out_shape=jax.ShapeDtypeStruct((B, N), F32),
        name="ada",
    )(c, w_ada, b_ada.reshape(1, N))


PROJ_COLS = 256


def _proj_kernel(x_ref, sc_ref, sh_ref, pos_ref, invf_ref, w_ref, rlb_ref,
                 q_o, kf_o, kb_o, lf_o, lb_o, v_o, g_o, aq_o, ak_o, av_o, gr_o, ga_o):
    u = (x_ref[0] * (1.0 + sc_ref[0]) + sh_ref[0]).astype(BF16)
    D = x_ref.shape[-1]
    W = REC_WIDTH

    def pieces(base, width):
        for c in range(0, width, PROJ_COLS):
            yield slice(c, c + PROJ_COLS), jnp.dot(
                u, w_ref[:, base + c:base + c + PROJ_COLS], preferred_element_type=F32)

    for sl, y in pieces(0, W):
        q_o[0, :, sl] = (_silu(y) * (REC_DK ** -0.5)).astype(BF16)

    for direction, (k_o, l_o) in enumerate(((kf_o, lf_o), (kb_o, lb_o))):
        r = rlb_ref[direction]
        e = jnp.exp(r - jnp.max(r, axis=0, keepdims=True))
        lbd = e[0:1] / jnp.sum(e, axis=0, keepdims=True)
        for sl, y in pieces((1 + direction) * W, W):
            f = lbd[:, sl] + (1.0 - lbd[:, sl]) * _sigmoid(y)
            k_o[0, :, sl] = (1.0 - f).astype(BF16)
            l_o[0, :, sl] = jnp.log(f)

    for sl, y in pieces(3 * W, W):
        v_o[0, :, sl] = y.astype(BF16)
    for sl, y in pieces(4 * W, W):
        g_o[0, :, sl] = _silu(y).astype(BF16)

    ang = pos_ref[0].astype(F32) * invf_ref[...]
    cos = jnp.cos(ang)
    sin = jnp.sin(ang)
    lane = lax.broadcasted_iota(jnp.int32, ang.shape, 1)
    first_half = (lane % ATT_HEAD_DIM) < (ATT_HEAD_DIM // 2)
    sin_signed = jnp.where(first_half, -sin, sin)

    def rope(t):
        partner = jnp.where(first_half,
                            pltpu.roll(t, LANES - ATT_HEAD_DIM // 2, axis=1),
                            pltpu.roll(t, ATT_HEAD_DIM // 2, axis=1))
        return t * cos + partner * sin_signed

    c0 = 5 * W
    for sl, y in pieces(c0, ATT_QWIDTH):
        for s in range(PROJ_COLS // LANES):
            lanes = slice(sl.start + s * LANES, sl.start + (s + 1) * LANES)
            aq_o[0, :, lanes] = (rope(y[:, s * LANES:(s + 1) * LANES])
                                 * (ATT_HEAD_DIM ** -0.5 * LOG2E)).astype(BF16)
    c0 += ATT_QWIDTH
    assert 2 * ATT_KVWIDTH == PROJ_COLS
    for _, y in pieces(c0, 2 * ATT_KVWIDTH):
        ak_o[0] = rope(y[:, :ATT_KVWIDTH]).astype(BF16)
        av_o[0] = y[:, ATT_KVWIDTH:].astype(BF16)
    c0 += 2 * ATT_KVWIDTH
    for sl, y in pieces(c0, D):
        gr_o[0, :, sl] = _sigmoid(y).astype(BF16)
    for sl, y in pieces(c0 + D, D):
        ga_o[0, :, sl] = _sigmoid(y).astype(BF16)


def _proj(x, sc1, sh1, positions, w_in_bf, rec_lower_bound, tm):
    B, S, D = x.shape
    d_in = w_in_bf.shape[1]
    half = ATT_HEAD_DIM // 2
    inv_freq = ROPE_THETA ** (-jnp.arange(half, dtype=F32) / half)
    invf = jnp.tile(inv_freq, LANES // half).reshape(1, LANES)
    W = REC_WIDTH
    tok = lambda b, t: (b, t, 0)
    per_b = lambda b, t: (b, 0, 0)
    outs = [(W, BF16), (W, BF16), (W, BF16), (W, F32), (W, F32), (W, BF16), (W, BF16),
            (ATT_QWIDTH, BF16), (ATT_KVWIDTH, BF16), (ATT_KVWIDTH, BF16), (D, BF16), (D, BF16)]
    return pl.pallas_call(
        _proj_kernel,
        grid=(B, S // tm),
        in_specs=[pl.BlockSpec((1, tm, D), tok),
                  pl.BlockSpec((1, 1, D), per_b),
                  pl.BlockSpec((1, 1, D), per_b),
                  pl.BlockSpec((1, tm, 1), tok),
                  pl.BlockSpec((1, LANES), lambda b, t: (0, 0)),
                  pl.BlockSpec((D, d_in), lambda b, t: (0, 0), pipeline_mode=pl.Buffered(1)),
                  pl.BlockSpec(rec_lower_bound.shape, lambda b, t: (0, 0, 0))],
        out_specs=[pl.BlockSpec((1, tm, w), tok) for w, _ in outs],
        out_shape=[jax.ShapeDtypeStruct((B, S, w), dt) for w, dt in outs],
        compiler_params=pltpu.CompilerParams(vmem_limit_bytes=VMEM_LIMIT),
        name="proj",
    )(x, sc1, sh1, positions.reshape(B, S, 1), invf, w_in_bf, rec_lower_bound)


def _rec_constants():
    C = REC_CHUNK
    nblk = 2 + len(REC_LEVELS)
    mf = np.zeros((nblk, C, C), np.float32)
    r = np.arange(C)
    for t in range(C):
        mf[0, t] = r <= t
        mf[1, t] = r > t
        for i, m in enumerate(REC_LEVELS):
            boundary = (t // (2 * m)) * 2 * m + m
            if t >= boundary:
                mf[2 + i, t] = (r >= boundary) & (r <= t)
            else:
                mf[2 + i, t] = (r > t) & (r <= boundary - 1)
    mb = mf[:, ::-1, ::-1]
    x = r[:, None] ^ r[None, :]
    lev = np.where(x > 0, np.floor(np.log2(np.maximum(x, 1))).astype(np.int32), -1)
    invalid = -2
    levf = np.where(r[:, None] >= r[None, :], lev, invalid).astype(np.int32)
    levb = np.where(r[:, None] <= r[None, :], lev, invalid).astype(np.int32)
    m3 = lambda m: np.concatenate([m.reshape(nblk * C, C)] * 3, axis=1)
    return (jnp.asarray(np.stack([m3(mf), m3(mb)]), BF16),
            jnp.asarray(np.stack([levf, levb])))


REC_HEADS_PER_STEP = 2
REC_UNROLL = 2


def _rec_kernel(q_ref, kf_ref, kb_ref, lf_ref, lb_ref, v_ref, g_ref, ng_ref, m_ref, lev_ref,
                o_ref, acc_ref, qt_ref, upd_ref, dec_ref):
    C = REC_CHUNK
    HP = REC_HEADS_PER_STEP
    S = q_ref.shape[1]
    n_chunks = S // C
    row = lax.broadcasted_iota(jnp.int32, (C, LANES), 0)
    nt = (((1,), (1,)), ((), ()))
    tn = (((0,), (0,)), ((), ()))
    k_refs = (kf_ref, kb_ref)
    l_refs = (lf_ref, lb_ref)

    def local(i, carry):
        U = REC_UNROLL
        units = [(c, d, h) for c in range(U) for d in range(2) for h in range(HP)]
        ns = {(c, d): (U * i + c if d == 0 else n_chunks - 1 - (U * i + c))
              for c in range(U) for d in range(2)}
        rows = {cd: pl.ds(pl.multiple_of(n * C, C), C) for cd, n in ns.items()}
        sls = [slice(h * LANES, (h + 1) * LANES) for h in range(HP)]
        xs = {}
        for cd in ns:
            lg = l_refs[cd[1]][0, rows[cd], :]
            hi = lg.astype(BF16)
            r1 = lg - hi.astype(F32)
            mid = r1.astype(BF16)
            lo = (r1 - mid.astype(F32)).astype(BF16)
            e = jnp.dot(m_ref[cd[1]], jnp.concatenate([hi, mid, lo], axis=0),
                        preferred_element_type=F32)
            xs[cd] = jnp.exp(e)
        ws, kts, qs, ks, vs = {}, {}, {}, {}, {}
        for c, d in ns:
            cd = (c, d)
            last = C - 1 if d == 0 else 0
            dec_ref[d, pl.ds(ns[cd], 1), :] = xs[cd][last:last + 1]
            for h in range(HP):
                un = (c, d, h)
                q = q_ref[0, rows[cd], sls[h]]
                k = k_refs[d][0, rows[cd], sls[h]]
                qs[un], ks[un], vs[un] = q, k, v_ref[0, rows[cd], sls[h]]
                qf = q.astype(F32)
                kf = k.astype(F32)
                qt_ref[d, rows[cd], sls[h]] = (qf * xs[cd][0:C, sls[h]]).astype(BF16)
                kts[un] = (kf * xs[cd][C:2 * C, sls[h]]).astype(BF16)
                for j, m in enumerate(REC_LEVELS):
                    q_side = ((row & m) != 0) if d == 0 else ((row & m) == 0)
                    ws[un, j] = (jnp.where(q_side, qf, kf)
                                 * xs[cd][(2 + j) * C:(3 + j) * C, sls[h]]).astype(BF16)
        als = {}
        for un in units:
            als[un, -1] = lax.dot_general(qs[un], ks[un], nt, preferred_element_type=F32)
            for j in range(len(REC_LEVELS)):
                als[un, j] = lax.dot_general(ws[un, j], ws[un, j], nt, preferred_element_type=F32)
        for un in units:
            c, d, h = un
            upd_ref[d, h, ns[c, d]] = lax.dot_general(vs[un], kts[un], tn,
                                                      preferred_element_type=F32)
        avs = {}
        for un in units:
            lev = lev_ref[un[1]]
            a = jnp.where(lev == -1, als[un, -1], 0.0)
            for j, m in enumerate(REC_LEVELS):
                a = jnp.where(lev == (m.bit_length() - 1), als[un, j], a)
            avs[un] = a.astype(BF16)
        for un in units:
            c, d, h = un
            acc_ref[d, rows[c, d], sls[h]] = jnp.dot(avs[un], vs[un], preferred_element_type=F32)
        return carry

    lax.fori_loop(0, n_chunks // REC_UNROLL, local, 0)

    for direction in range(2):
        def carried(i, sts, direction=direction):
            n = i if direction == 0 else n_chunks - 1 - i
            rows = pl.ds(pl.multiple_of(n * C, C), C)
            d = dec_ref[direction, pl.ds(n, 1), :]
            out = []
            for h in range(HP):
                sl = slice(h * LANES, (h + 1) * LANES)
                st = sts[h]
                acc_ref[direction, rows, sl] += lax.dot_general(
                    qt_ref[direction, rows, sl], st.astype(BF16), nt, preferred_element_type=F32)
                out.append(st * d[:, sl] + upd_ref[direction, h, n])
            return tuple(out)

        lax.fori_loop(0, n_chunks, carried,
                      tuple(jnp.zeros((REC_DV, REC_DK), F32) for _ in range(HP)), unroll=4)

    RB = 256

    def finish(i, carry):
        rows = pl.ds(pl.multiple_of(i * RB, RB), RB)
        for h in range(HP):
            sl = slice(h * LANES, (h + 1) * LANES)
            o = acc_ref[0, rows, sl] + acc_ref[1, rows, sl]
            o = o * lax.rsqrt(jnp.mean(o * o, axis=-1, keepdims=True) + RMS_EPS) * ng_ref[...]
            o_ref[0, rows, sl] = (o * g_ref[0, rows, sl].astype(F32)).astype(BF16)
        return carry

    lax.fori_loop(0, S // RB, finish, 0)


def _rec(q, kf, kb, lf, lb, v, g, norm_g):
    B, S, W = q.shape
    HP = REC_HEADS_PER_STEP
    m3, lev = _rec_constants()
    n_chunks = S // REC_CHUNK
    head = lambda b, h: (b, 0, h)
    blk = pl.BlockSpec((1, S, HP * LANES), head)
    return pl.pallas_call(
        _rec_kernel,
        grid=(B, REC_HEADS // HP),
        in_specs=[blk, blk, blk, blk, blk, blk, blk,
                  pl.BlockSpec((1, REC_DV), lambda b, h: (0, 0)),
                  pl.BlockSpec(m3.shape, lambda b, h: (0, 0, 0)),
                  pl.BlockSpec(lev.shape, lambda b, h: (0, 0, 0))],
        out_specs=blk,
        out_shape=jax.ShapeDtypeStruct((B, S, W), BF16),
        scratch_shapes=[pltpu.VMEM((2, S, HP * REC_DV), F32),
                        pltpu.VMEM((2, S, HP * REC_DK), BF16),
                        pltpu.VMEM((2, HP, n_chunks, REC_DV, REC_DK), F32),
                        pltpu.VMEM((2, n_chunks, HP * REC_DK), F32)],
        compiler_params=pltpu.CompilerParams(vmem_limit_bytes=VMEM_LIMIT),
        name="rec",
    )(q, kf, kb, lf, lb, v, g, norm_g.reshape(1, REC_DV), m3, lev)


def _att_bias(S):
    rowi = np.arange(ATT_BLOCK)[:, None]
    coli = np.arange(3 * ATT_BLOCK)[None, :]
    out = []
    for off in (0, ATT_BLOCK, 2 * ATT_BLOCK):
        ok = np.abs(coli - off - rowi) <= ATT_WINDOW
        out.append(np.where(ok, 0.0, NEG_BIG))
    return jnp.asarray(np.stack(out), F32)


def _mix_kernel(x_ref, ga_ref, sc_ref, sh_ref, aq_ref, ak_ref, av_ref, orec_ref, gr_ref, gt_ref,
                sink_ref, bias_ref, wrec_ref, watt_ref, wout_ref, lg_ref, lb_ref,
                x1_o, u2_o, kvar, vvar, att_scr, yrec_scr):
    t = pl.program_id(1)
    S = ak_ref.shape[1]
    TQ = x_ref.shape[1]
    nb = S // ATT_BLOCK
    KW = 3 * ATT_BLOCK
    HALF = LANES // 2
    nt = (((1,), (1,)), ((), ()))

    @pl.when(t == 0)
    def _():
        lane = lax.broadcasted_iota(jnp.int32, (S, LANES), 1)
        lo = lane < HALF
        k = ak_ref[0].astype(F32)
        v = av_ref[0].astype(F32)
        ksw = pltpu.roll(k, HALF, axis=1)
        vsw = pltpu.roll(v, HALF, axis=1)
        zero = jnp.zeros_like(k)
        for i, (kk, vv) in enumerate([(jnp.where(lo, k, zero), jnp.where(lo, v, zero)),
                                      (jnp.where(lo, zero, ksw), jnp.where(lo, zero, vsw)),
                                      (jnp.where(lo, ksw, zero), jnp.where(lo, vsw, zero)),
                                      (jnp.where(lo, zero, k), jnp.where(lo, zero, v))]):
            kvar[i] = kk.astype(BF16)
            vvar[i, :, 0:LANES] = vv.astype(BF16)
            ones = jnp.where(lo, 1.0, 0.0) if i % 2 == 0 else jnp.where(lo, 0.0, 1.0)
            vvar[i, :, LANES:2 * LANES] = ones.astype(BF16)

    lane_q = lax.broadcasted_iota(jnp.int32, (ATT_BLOCK, LANES), 1)
    lo_q = lane_q < HALF
    n_blocks = TQ // ATT_BLOCK

    def attend(jj):
        rows = slice(jj * ATT_BLOCK, (jj + 1) * ATT_BLOCK)
        j = t * n_blocks + jj
        ks = pl.multiple_of(jnp.clip((j - 1) * ATT_BLOCK, 0, S - KW), ATT_BLOCK)
        bidx = jnp.where(j == 0, 0, jnp.where(j == nb - 1, 2, 1))
        bias = bias_ref[bidx]
        win = pl.ds(ks, KW)
        for s in range(ATT_QWIDTH // LANES):
            hk = (2 * s) // (ATT_Q_HEADS // ATT_KV_HEADS)
            qs = aq_ref[0, rows, s * LANES:(s + 1) * LANES]
            acc = jnp.zeros((ATT_BLOCK, 2 * LANES), F32)
            den_sink = []
            for par in range(2):
                sc = lax.dot_general(qs, kvar[2 * hk + par, win, :], nt,
                                     preferred_element_type=F32) + bias
                sk = sink_ref[0, 2 * s + par] * LOG2E
                mx = jnp.maximum(jnp.max(sc, axis=-1, keepdims=True), sk)
                p = jnp.exp2(sc - mx).astype(BF16)
                acc = acc + jnp.dot(p, vvar[2 * hk + par, win, :], preferred_element_type=F32)
                den_sink.append(jnp.exp2(sk - mx))
            den = acc[:, LANES:] + jnp.where(lo_q, den_sink[0], den_sink[1])
            att_scr[rows, s * LANES:(s + 1) * LANES] = (acc[:, :LANES] / den).astype(BF16)

    def dense(jj):
        rows = slice(jj * ATT_BLOCK, (jj + 1) * ATT_BLOCK)
        y_att = jnp.dot(att_scr[rows, :], watt_ref[...], preferred_element_type=F32)
        merged = (gr_ref[0, rows, :].astype(F32) * yrec_scr[rows, :]
                  + gt_ref[0, rows, :].astype(F32) * y_att)
        z = jnp.dot(merged.astype(BF16), wout_ref[...], preferred_element_type=F32)
        r = DEEPNORM_ALPHA * x_ref[0, rows, :] + (1.0 + ga_ref[0]) * z
        x1 = _layer_norm(r, lg_ref[...], lb_ref[...])
        x1_o[0, rows, :] = x1
        u2_o[0, rows, :] = (x1 * (1.0 + sc_ref[0]) + sh_ref[0]).astype(BF16)

    yrec_scr[...] = jnp.dot(orec_ref[0], wrec_ref[...], preferred_element_type=F32)
    attend(0)
    for jj in range(n_blocks):
        if jj + 1 < n_blocks:
            attend(jj + 1)
        dense(jj)


def _mix(x, ga1, sc2, sh2, aq, ak, av, orec, g_rec, g_att, sink, w_rec, w_att, w_out, ln_g, ln_b, tq):
    B, S, D = x.shape
    bias = _att_bias(S)
    tok = lambda b, t: (b, t, 0)
    per_b = lambda b, t: (b, 0, 0)
    c2 = lambda b, t: (0, 0)
    full = lambda a: pl.BlockSpec(a.shape, (lambda b, t: (0,) * a.ndim))
    return pl.pallas_call(
        _mix_kernel,
        grid=(B, S // tq),
        in_specs=[pl.BlockSpec((1, tq, D), tok),
                  pl.BlockSpec((1, 1, D), per_b), pl.BlockSpec((1, 1, D), per_b),
                  pl.BlockSpec((1, 1, D), per_b),
                  pl.BlockSpec((1, tq, ATT_QWIDTH), tok),
                  pl.BlockSpec((1, S, ATT_KVWIDTH), per_b),
                  pl.BlockSpec((1, S, ATT_KVWIDTH), per_b),
                  pl.BlockSpec((1, tq, REC_WIDTH), tok),
                  pl.BlockSpec((1, tq, D), tok), pl.BlockSpec((1, tq, D), tok),
                  pl.BlockSpec(memory_space=pltpu.SMEM),
                  full(bias), full(w_rec), full(w_att), full(w_out),
                  pl.BlockSpec((1, D), c2), pl.BlockSpec((1, D), c2)],
        out_specs=[pl.BlockSpec((1, tq, D), tok), pl.BlockSpec((1, tq, D), tok)],
        out_shape=[jax.ShapeDtypeStruct((B, S, D), F32), jax.ShapeDtypeStruct((B, S, D), BF16)],
        scratch_shapes=[pltpu.VMEM((4, S, LANES), BF16),
                        pltpu.VMEM((4, S, 2 * LANES), BF16),
                        pltpu.VMEM((tq, ATT_QWIDTH), BF16),
                        pltpu.VMEM((tq, D), F32)],
        compiler_params=pltpu.CompilerParams(
            dimension_semantics=("arbitrary", "arbitrary"), vmem_limit_bytes=VMEM_LIMIT),
        name="mix",
    )(x, ga1, sc2, sh2, aq, ak, av, orec, g_rec, g_att, sink.reshape(1, ATT_Q_HEADS), bias,
      w_rec, w_att, w_out, ln_g.reshape(1, D), ln_b.reshape(1, D))


HALO = 16
FFN_CHUNK = 256
FFN_ROWS = 128


def _gelu_tanh(x):
    return 0.5 * x * (1.0 + jnp.tanh(np.sqrt(2.0 / np.pi).astype(np.float32) * (x + 0.044715 * (x * x * x))))


def _ffn_kernel(u_ref, up_ref, un_ref, x1_ref, ga_ref, wup_ref, cw_ref, cb_ref, wdn_ref,
                lg_ref, lb_ref, o_ref, ucat, h_scr, a_scr, acc_ref):
    t = pl.program_id(1)
    nt_ = pl.num_programs(1)
    tm = u_ref.shape[1]
    d_ff = wdn_ref.shape[0]
    ucat[0:HALO] = jnp.where(t > 0, up_ref[0], jnp.zeros_like(up_ref[0]))
    ucat[HALO:HALO + tm] = u_ref[0]
    ucat[HALO + tm:] = jnp.where(t < nt_ - 1, un_ref[0], jnp.zeros_like(un_ref[0]))
    u = ucat[...]

    n_chunks = d_ff // FFN_CHUNK

    def up(j, part):
        if j < n_chunks:
            col = part * d_ff + j * FFN_CHUNK
            h = jnp.dot(u, wup_ref[:, col:col + FFN_CHUNK], preferred_element_type=F32)
            for s in range(FFN_CHUNK // LANES):
                h_scr[j % 2, part, s] = h[:, s * LANES:(s + 1) * LANES]

    def conv(j, part, r0):
        slabs = []
        for s in range(FFN_CHUNK // LANES):
            col = part * d_ff + j * FFN_CHUNK + s * LANES
            w = cw_ref[:, col:col + LANES]
            out = cb_ref[:, col:col + LANES]
            for tap in range(CONV_K):
                lo = HALO + tap - CONV_K // 2 + r0
                out = out + h_scr[j % 2, part, s, lo:lo + FFN_ROWS, :] * w[tap:tap + 1]
            slabs.append(out)
        return jnp.concatenate(slabs, axis=1)

    def down(j, r0):
        rows = slice(r0, r0 + FFN_ROWS)
        slot = (r0 // FFN_ROWS) % a_scr.shape[0]
        a_scr[slot] = (_gelu_tanh(conv(j, 1, r0)) * conv(j, 0, r0)).astype(BF16)
        part = jnp.dot(a_scr[slot], wdn_ref[j * FFN_CHUNK:(j + 1) * FFN_CHUNK, :],
                       preferred_element_type=F32)
        if j == 0:
            acc_ref[rows, :] = part
        elif j < n_chunks - 1:
            acc_ref[rows, :] += part
        else:
            r = DEEPNORM_ALPHA * x1_ref[0, rows, :] + (1.0 + ga_ref[0]) * (acc_ref[rows, :] + part)
            o_ref[0, rows, :] = _layer_norm(r, lg_ref[...], lb_ref[...])

    up(0, 0)
    up(0, 1)
    row_blocks = list(range(0, tm, FFN_ROWS))
    half = len(row_blocks) // 2
    for j in range(n_chunks):
        up(j + 1, 0)
        for r0 in row_blocks[:half]:
            down(j, r0)
        up(j + 1, 1)
        for r0 in row_blocks[half:]:
            down(j, r0)


def _ffn(u2, x1, ga2, w_up, conv_w, conv_b, w_down, ln_g, ln_b, tm):
    B, S, D = x1.shape
    d_ff = w_down.shape[0]
    assert d_ff % FFN_CHUNK == 0 and tm % HALO == 0
    nh = tm // HALO
    last = S // HALO - 1
    tok = lambda b, t: (b, t, 0)
    per_b = lambda b, t: (b, 0, 0)
    c2 = lambda b, t: (0, 0)
    return pl.pallas_call(
        _ffn_kernel,
        grid=(B, S // tm),
        in_specs=[pl.BlockSpec((1, tm, D), tok),
                  pl.BlockSpec((1, HALO, D), lambda b, t: (b, jnp.maximum(t * nh - 1, 0), 0)),
                  pl.BlockSpec((1, HALO, D), lambda b, t: (b, jnp.minimum((t + 1) * nh, last), 0)),
                  pl.BlockSpec((1, tm, D), tok),
                  pl.BlockSpec((1, 1, D), per_b),
                  pl.BlockSpec(w_up.shape, c2, pipeline_mode=pl.Buffered(1)),
                  pl.BlockSpec(conv_w.shape, c2),
                  pl.BlockSpec((1, 2 * d_ff), c2),
                  pl.BlockSpec(w_down.shape, c2, pipeline_mode=pl.Buffered(1)),
                  pl.BlockSpec((1, D), c2), pl.BlockSpec((1, D), c2)],
        out_specs=pl.BlockSpec((1, tm, D), tok),
        out_shape=jax.ShapeDtypeStruct((B, S, D), F32),
        scratch_shapes=[pltpu.VMEM((tm + 2 * HALO, D), BF16),
                        pltpu.VMEM((2, 2, FFN_CHUNK // LANES, tm + 2 * HALO, LANES), F32),
                        pltpu.VMEM((tm // FFN_ROWS, FFN_ROWS, FFN_CHUNK), BF16),
                        pltpu.VMEM((tm, D), F32)],
        compiler_params=pltpu.CompilerParams(vmem_limit_bytes=VMEM_LIMIT),
        name="ffn",
    )(u2, u2, u2, x1, ga2, w_up, conv_w, conv_b.reshape(1, 2 * d_ff), w_down,
      ln_g.reshape(1, D), ln_b.reshape(1, D))


def kernel(x, c, positions, w_ada, b_ada, w_in, rec_lower_bound, rec_norm_g, w_rec_branch,
           attn_sink, w_attn_branch, w_out, ln1_g, ln1_b, w_up, conv_w, conv_b, w_down,
           ln2_g, ln2_b):
    B, S, D = x.shape
    assert w_ada.shape[0] == DEPTH and rec_lower_bound.shape[1] == DEPTH + 1
    tm_proj = min(512, S)
    tq = min(512, S)
    tm_ffn = min(512, S)
    for l in range(DEPTH):
        mods = _ada(c, w_ada[l], b_ada[l])
        sh1, sc1, ga1, sh2, sc2, ga2 = [m.reshape(B, 1, D) for m in jnp.split(mods, 6, axis=-1)]
        (q, kf, kb, lf, lb, v, g, aq, ak, av, g_rec, g_att) = _proj(
            x, sc1, sh1, positions, w_in[l].astype(BF16), rec_lower_bound, tm_proj)
        orec = _rec(q, kf, kb, lf, lb, v, g, rec_norm_g[l])
        x1, u2 = _mix(x, ga1, sc2, sh2, aq, ak, av, orec, g_rec, g_att, attn_sink[l],
                      w_rec_branch[l].astype(BF16), w_attn_branch[l].astype(BF16),
                      w_out[l].astype(BF16), ln1_g[l], ln1_b[l], tq)
        x = _ffn(u2, x1, ga2, w_up[l].astype(BF16), conv_w[l], conv_b[l], w_down[l].astype(BF16),
                 ln2_g[l], ln2_b[l], tm_ffn)
    return x
```

```python
import numpy as np
import jax
import jax.numpy as jnp
from jax import lax
from jax.experimental import pallas as pl
from jax.experimental.pallas import tpu as pltpu

F32 = jnp.float32
BF16 = jnp.bfloat16

REC_HEADS = 4
REC_DK = 128
REC_DV = 128
REC_WIDTH = REC_HEADS * REC_DK
ATT_Q_HEADS = 8
ATT_KV_HEADS = 2
ATT_HEAD_DIM = 64
ATT_QWIDTH = ATT_Q_HEADS * ATT_HEAD_DIM
ATT_KVWIDTH = ATT_KV_HEADS * ATT_HEAD_DIM
ATT_WINDOW = 128
ATT_BLOCK = 128
ROPE_THETA = 10000.0
CONV_K = 3
LN_EPS = 1e-5
RMS_EPS = 1e-6
DEPTH = 1
DEEPNORM_ALPHA = (2 * DEPTH) ** 0.25

LANES = 128
REC_CHUNK = 64
REC_LEVELS = (32, 16, 8, 4, 2, 1)
REC_MXU_LEVELS = (4, 2, 1)
LOG2E = 1.4426950408889634
NEG_BIG = -1e30
VMEM_LIMIT = 56 * 1024 * 1024


def _sigmoid(x):
    return 1.0 / (1.0 + jnp.exp(-x))


def _silu(x):
    return x * _sigmoid(x)


def _layer_norm(r, g, b):
    mu = jnp.mean(r, axis=-1, keepdims=True)
    rc = r - mu
    var = jnp.mean(rc * rc, axis=-1, keepdims=True)
    return rc * lax.rsqrt(var + LN_EPS) * g + b


def _ada_kernel(c_ref, w_ref, b_ref, o_ref):
    c = c_ref[...]
    o_ref[...] = jnp.dot(_silu(c), w_ref[...], preferred_element_type=F32) + b_ref[...]


def _ada(c, w_ada, b_ada):
    B, D = c.shape
    N = w_ada.shape[1]
    return pl.pallas_call(
        _ada_kernel,
        grid=(N // D,),
        in_specs=[pl.BlockSpec((B, D), lambda j: (0, 0)),
                  pl.BlockSpec((D, D), lambda j: (0, j)),
                  pl.BlockSpec((1, D), lambda j: (0, j))],
        out_specs=pl.BlockSpec((B, D), lambda j: (0, j)),
        out_shape=jax.ShapeDtypeStruct((B, N), F32),
        name="ada",
    )(c, w_ada, b_ada.reshape(1, N))


PROJ_COLS = 256


def _proj_kernel(x_ref, sc_ref, sh_ref, pos_ref, invf_ref, w_ref, rlb_ref,
                 q_o, kf_o, kb_o, lfh_o, lfl_o, lbh_o, lbl_o, v_o, g_o, aq_o, ak_o, av_o, gr_o, ga_o):
    u = (x_ref[0] * (1.0 + sc_ref[0]) + sh_ref[0]).astype(BF16)
    D = x_ref.shape[-1]
    W = REC_WIDTH

    def pieces(base, width):
        for c in range(0, width, PROJ_COLS):
            yield slice(c, c + PROJ_COLS), jnp.dot(
                u, w_ref[:, base + c:base + c + PROJ_COLS], preferred_element_type=F32)

    for sl, y in pieces(0, W):
        q_o[0, :, sl] = (_silu(y) * (REC_DK ** -0.5)).astype(BF16)

    for direction, (k_o, hi_o, lo_o) in enumerate(((kf_o, lfh_o, lfl_o), (kb_o, lbh_o, lbl_o))):
        r = rlb_ref[direction]
        e = jnp.exp(r - jnp.max(r, axis=0, keepdims=True))
        lbd = e[0:1] / jnp.sum(e, axis=0, keepdims=True)
        for sl, y in pieces((1 + direction) * W, W):
            f = lbd[:, sl] + (1.0 - lbd[:, sl]) * _sigmoid(y)
            k_o[0, :, sl] = (1.0 - f).astype(BF16)
            lg = jnp.log(f)
            hi = lg.astype(BF16)
            hi_o[0, :, sl] = hi
            lo_o[0, :, sl] = (lg - hi.astype(F32)).astype(BF16)

    for sl, y in pieces(3 * W, W):
        v_o[0, :, sl] = y.astype(BF16)
    for sl, y in pieces(4 * W, W):
        g_o[0, :, sl] = _silu(y).astype(BF16)

    ang = pos_ref[0].astype(F32) * invf_ref[...]
    cos = jnp.cos(ang)
    sin = jnp.sin(ang)
    lane = lax.broadcasted_iota(jnp.int32, ang.shape, 1)
    first_half = (lane % ATT_HEAD_DIM) < (ATT_HEAD_DIM // 2)
    sin_signed = jnp.where(first_half, -sin, sin)

    def rope(t):
        partner = jnp.where(first_half,
                            pltpu.roll(t, LANES - ATT_HEAD_DIM // 2, axis=1),
                            pltpu.roll(t, ATT_HEAD_DIM // 2, axis=1))
        return t * cos + partner * sin_signed

    c0 = 5 * W
    for sl, y in pieces(c0, ATT_QWIDTH):
        for s in range(PROJ_COLS // LANES):
            lanes = slice(sl.start + s * LANES, sl.start + (s + 1) * LANES)
            aq_o[0, :, lanes] = (rope(y[:, s * LANES:(s + 1) * LANES])
                                 * (ATT_HEAD_DIM ** -0.5 * LOG2E)).astype(BF16)
    c0 += ATT_QWIDTH
    assert 2 * ATT_KVWIDTH == PROJ_COLS
    for _, y in pieces(c0, 2 * ATT_KVWIDTH):
        ak_o[0] = rope(y[:, :ATT_KVWIDTH]).astype(BF16)
        av_o[0] = y[:, ATT_KVWIDTH:].astype(BF16)
    c0 += 2 * ATT_KVWIDTH
    for sl, y in pieces(c0, D):
        gr_o[0, :, sl] = _sigmoid(y).astype(BF16)
    for sl, y in pieces(c0 + D, D):
        ga_o[0, :, sl] = _sigmoid(y).astype(BF16)


def _proj(x, sc1, sh1, positions, w_in_bf, rec_lower_bound, tm):
    B, S, D = x.shape
    d_in = w_in_bf.shape[1]
    half = ATT_HEAD_DIM // 2
    inv_freq = ROPE_THETA ** (-jnp.arange(half, dtype=F32) / half)
    invf = jnp.tile(inv_freq, LANES // half).reshape(1, LANES)
    W = REC_WIDTH
    tok = lambda b, t: (b, t, 0)
    per_b = lambda b, t: (b, 0, 0)
    outs = [W] * 9 + [ATT_QWIDTH, ATT_KVWIDTH, ATT_KVWIDTH, D, D]
    return pl.pallas_call(
        _proj_kernel,
        grid=(B, S // tm),
        in_specs=[pl.BlockSpec((1, tm, D), tok),
                  pl.BlockSpec((1, 1, D), per_b),
                  pl.BlockSpec((1, 1, D), per_b),
                  pl.BlockSpec((1, tm, 1), tok),
                  pl.BlockSpec((1, LANES), lambda b, t: (0, 0)),
                  pl.BlockSpec((D, d_in), lambda b, t: (0, 0), pipeline_mode=pl.Buffered(1)),
                  pl.BlockSpec(rec_lower_bound.shape, lambda b, t: (0, 0, 0))],
        out_specs=[pl.BlockSpec((1, tm, w), tok) for w in outs],
        out_shape=[jax.ShapeDtypeStruct((B, S, w), BF16) for w in outs],
        compiler_params=pltpu.CompilerParams(vmem_limit_bytes=VMEM_LIMIT),
        name="proj",
    )(x, sc1, sh1, positions.reshape(B, S, 1), invf, w_in_bf, rec_lower_bound)


REC_HEADS_PER_STEP = 2
REC_UNROLL = 2


def _rec_constants():
    C = REC_CHUNK
    r = np.arange(C)
    blocks = [r[None, :] <= r[:, None]]
    for m in REC_MXU_LEVELS:
        boundary = (r // (2 * m)) * 2 * m + m
        q_side = r >= boundary
        blocks.append(np.where(q_side[:, None],
                               (r[None, :] >= boundary[:, None]) & (r[None, :] <= r[:, None]),
                               (r[None, :] > r[:, None]) & (r[None, :] <= boundary[:, None] - 1)))
    mf = np.stack(blocks).astype(np.float32)
    mb = mf[:, ::-1, ::-1]
    x = r[:, None] ^ r[None, :]
    lev = np.where(x > 0, np.floor(np.log2(np.maximum(x, 1))).astype(np.int32), -1)
    invalid = -2
    levf = np.where(r[:, None] >= r[None, :], lev, invalid).astype(np.int32)
    levb = np.where(r[:, None] <= r[None, :], lev, invalid).astype(np.int32)
    m2 = lambda m: np.concatenate([m.reshape(-1, C)] * 2, axis=1)
    return (jnp.asarray(np.stack([m2(mf), m2(mb)]), BF16),
            jnp.asarray(np.stack([np.tile(levf, (1, REC_HEADS_PER_STEP)),
                                  np.tile(levb, (1, REC_HEADS_PER_STEP))])))


def _rec_kernel(q_ref, kf_ref, kb_ref, lfh_ref, lfl_ref, lbh_ref, lbl_ref, v_ref, g_ref, ng_ref,
                m_ref, lev_ref, o_ref, acc_ref, qt_ref, upd_ref, dec_ref, w_scr):
    C = REC_CHUNK
    HP = REC_HEADS_PER_STEP
    S = q_ref.shape[1]
    n_chunks = S // C
    row = lax.broadcasted_iota(jnp.int32, (C, HP * LANES), 0)
    nt = (((1,), (1,)), ((), ()))
    tn = (((0,), (0,)), ((), ()))
    k_refs = (kf_ref, kb_ref)
    l_refs = ((lfh_ref, lfl_ref), (lbh_ref, lbl_ref))

    U = REC_UNROLL
    units = [(c, d) for c in range(U) for d in range(2)]
    zero = jnp.zeros((C, LANES), BF16)

    def chunk_rows(g, cd):
        c, d = cd
        n = U * g + c if d == 0 else n_chunks - 1 - (U * g + c)
        return n, pl.ds(pl.multiple_of(n * C, C), C)

    def block_diag(t):
        return jnp.concatenate(
            [jnp.concatenate([t[:, g * LANES:(g + 1) * LANES] if g == h else zero
                              for g in range(HP)], axis=1) for h in range(HP)], axis=0)

    def exponents(g):
        es = {}
        for cd in units:
            d = cd[1]
            _, rows = chunk_rows(g, cd)
            hi_ref, lo_ref = l_refs[d]
            l2 = jnp.concatenate([hi_ref[0, rows, :], lo_ref[0, rows, :]], axis=0)
            es[cd] = jnp.dot(m_ref[d], l2, preferred_element_type=F32)
        return es

    def operands(g, es):
        for cd in units:
            c, d = cd
            n, rows = chunk_rows(g, cd)
            e = es[cd]
            b = e[0:C]
            last = C - 1 if d == 0 else 0
            ex = {"q": b, "k": b[last:last + 1] - b}
            for j, m in enumerate(REC_MXU_LEVELS):
                ex[m] = e[(1 + j) * C:(2 + j) * C]
            for m in REC_LEVELS:
                if m in REC_MXU_LEVELS:
                    continue
                parts = []
                for r0 in range(0, C, 8):
                    base = (r0 // (2 * m)) * 2 * m + m
                    ref_row = base - 1 if d == 0 else base
                    q_side = ((r0 & m) != 0) if d == 0 else ((r0 & m) == 0)
                    ref = b[ref_row:ref_row + 1]
                    parts.append(b[r0:r0 + 8] - ref if q_side else ref - b[r0:r0 + 8])
                ex[m] = jnp.concatenate(parts, axis=0)
            x = {key: jnp.exp(val) for key, val in ex.items()}
            dec_ref[d, pl.ds(n, 1), :] = x["q"][last:last + 1]
            qf = q_ref[0, rows, :].astype(F32)
            kf = k_refs[d][0, rows, :].astype(F32)
            qt_ref[d, rows, :] = (qf * x["q"]).astype(BF16)
            w_scr[c, d, 0] = (kf * x["k"]).astype(BF16)
            for j, m in enumerate(REC_LEVELS):
                q_side = ((row & m) != 0) if d == 0 else ((row & m) == 0)
                w_scr[c, d, 1 + j] = (jnp.where(q_side, qf, kf) * x[m]).astype(BF16)

    def scores(g):
        als, vs, meta = {}, {}, {}
        for cd in units:
            c, d = cd
            n, rows = chunk_rows(g, cd)
            meta[cd] = (n, rows)
            vs[cd] = v_ref[0, rows, :]
            als[cd, -1] = lax.dot_general(q_ref[0, rows, :], block_diag(k_refs[d][0, rows, :]), nt,
                                          preferred_element_type=F32)
            for j in range(len(REC_LEVELS)):
                w = w_scr[c, d, 1 + j]
                als[cd, j] = lax.dot_general(w, block_diag(w), nt, preferred_element_type=F32)
        return als, vs, meta

    def state_updates(vs, meta):
        for cd in units:
            c, d = cd
            kt = w_scr[c, d, 0]
            for h in range(HP):
                sl = slice(h * LANES, (h + 1) * LANES)
                upd_ref[d, h, meta[cd][0]] = lax.dot_general(vs[cd][:, sl], kt[:, sl], tn,
                                                             preferred_element_type=F32)

    def outputs(als, vs, meta):
        avs = {}
        for cd in units:
            lev = lev_ref[cd[1]]
            a = jnp.where(lev == -1, als[cd, -1], 0.0)
            for j, m in enumerate(REC_LEVELS):
                a = jnp.where(lev == (m.bit_length() - 1), als[cd, j], a)
            avs[cd] = a.astype(BF16)
        for cd in units:
            acc_ref[cd[1], meta[cd][1], :] = jnp.dot(avs[cd], block_diag(vs[cd]),
                                                     preferred_element_type=F32)

    def local(g, carry):
        operands(g, exponents(g))
        als, vs, meta = scores(g)
        state_updates(vs, meta)
        outputs(als, vs, meta)
        return carry

    lax.fori_loop(0, n_chunks // U, local, 0)

    for direction in range(2):
        def carried(i, sts, direction=direction):
            n = i if direction == 0 else n_chunks - 1 - i
            rows = pl.ds(pl.multiple_of(n * C, C), C)
            d = dec_ref[direction, pl.ds(n, 1), :]
            out = []
            for h in range(HP):
                sl = slice(h * LANES, (h + 1) * LANES)
                st = sts[h]
                acc_ref[direction, rows, sl] += lax.dot_general(
                    qt_ref[direction, rows, sl], st.astype(BF16), nt, preferred_element_type=F32)
                out.append(st * d[:, sl] + upd_ref[direction, h, n])
            return tuple(out)

        lax.fori_loop(0, n_chunks, carried,
                      tuple(jnp.zeros((REC_DV, REC_DK), F32) for _ in range(HP)), unroll=4)

    RB = 256

    def finish(i, carry):
        rows = pl.ds(pl.multiple_of(i * RB, RB), RB)
        for h in range(HP):
            sl = slice(h * LANES, (h + 1) * LANES)
            o = acc_ref[0, rows, sl] + acc_ref[1, rows, sl]
            o = o * lax.rsqrt(jnp.mean(o * o, axis=-1, keepdims=True) + RMS_EPS) * ng_ref[...]
            o_ref[0, rows, sl] = (o * g_ref[0, rows, sl].astype(F32)).astype(BF16)
        return carry

    lax.fori_loop(0, S // RB, finish, 0)


def _rec(q, kf, kb, lfh, lfl, lbh, lbl, v, g, norm_g):
    B, S, W = q.shape
    HP = REC_HEADS_PER_STEP
    m2, lev = _rec_constants()
    n_chunks = S // REC_CHUNK
    head = lambda b, h: (b, 0, h)
    blk = pl.BlockSpec((1, S, HP * LANES), head)
    return pl.pallas_call(
        _rec_kernel,
        grid=(B, REC_HEADS // HP),
        in_specs=[blk] * 9 + [
                  pl.BlockSpec((1, REC_DV), lambda b, h: (0, 0)),
                  pl.BlockSpec(m2.shape, lambda b, h: (0, 0, 0)),
                  pl.BlockSpec(lev.shape, lambda b, h: (0, 0, 0))],
        out_specs=blk,
        out_shape=jax.ShapeDtypeStruct((B, S, W), BF16),
        scratch_shapes=[pltpu.VMEM((2, S, HP * REC_DV), F32),
                        pltpu.VMEM((2, S, HP * REC_DK), BF16),
                        pltpu.VMEM((2, HP, n_chunks, REC_DV, REC_DK), F32),
                        pltpu.VMEM((2, n_chunks, HP * REC_DK), F32),
                        pltpu.VMEM((REC_UNROLL, 2, 1 + len(REC_LEVELS), REC_CHUNK, HP * REC_DK),
                                   BF16)],
        compiler_params=pltpu.CompilerParams(vmem_limit_bytes=VMEM_LIMIT),
        name="rec",
    )(q, kf, kb, lfh, lfl, lbh, lbl, v, g, norm_g.reshape(1, REC_DV), m2, lev)


def _att_bias(S):
    rowi = np.arange(ATT_BLOCK)[:, None]
    coli = np.arange(3 * ATT_BLOCK)[None, :]
    out = []
    for off in (0, ATT_BLOCK, 2 * ATT_BLOCK):
        ok = np.abs(coli - off - rowi) <= ATT_WINDOW
        out.append(np.where(ok, 0.0, NEG_BIG))
    return jnp.asarray(np.stack(out), F32)


def _mix_kernel(x_ref, ga_ref, sc_ref, sh_ref, aq_ref, ak_ref, av_ref, orec_ref, gr_ref, gt_ref,
                sink_ref, bias_ref, wrec_ref, watt_ref, wout_ref, lg_ref, lb_ref,
                x1_o, u2_o, kvar, vvar, att_scr, yrec_scr):
    t = pl.program_id(1)
    S = ak_ref.shape[1]
    TQ = x_ref.shape[1]
    nb = S // ATT_BLOCK
    KW = 3 * ATT_BLOCK
    HALF = LANES // 2
    nt = (((1,), (1,)), ((), ()))

    @pl.when(t == 0)
    def _():
        lane = lax.broadcasted_iota(jnp.int32, (S, LANES), 1)
        lo = lane < HALF
        k = ak_ref[0].astype(F32)
        v = av_ref[0].astype(F32)
        ksw = pltpu.roll(k, HALF, axis=1)
        vsw = pltpu.roll(v, HALF, axis=1)
        zero = jnp.zeros_like(k)
        for i, (kk, vv) in enumerate([(jnp.where(lo, k, zero), jnp.where(lo, v, zero)),
                                      (jnp.where(lo, zero, ksw), jnp.where(lo, zero, vsw)),
                                      (jnp.where(lo, ksw, zero), jnp.where(lo, vsw, zero)),
                                      (jnp.where(lo, zero, k), jnp.where(lo, zero, v))]):
            kvar[i] = kk.astype(BF16)
            vvar[i, :, 0:LANES] = vv.astype(BF16)
            ones = jnp.where(lo, 1.0, 0.0) if i % 2 == 0 else jnp.where(lo, 0.0, 1.0)
            vvar[i, :, LANES:2 * LANES] = ones.astype(BF16)

    lane_q = lax.broadcasted_iota(jnp.int32, (ATT_BLOCK, LANES), 1)
    lo_q = lane_q < HALF
    n_blocks = TQ // ATT_BLOCK

    def attend_scores(jj):
        rows = slice(jj * ATT_BLOCK, (jj + 1) * ATT_BLOCK)
        j = t * n_blocks + jj
        ks = pl.multiple_of(jnp.clip((j - 1) * ATT_BLOCK, 0, S - KW), ATT_BLOCK)
        bidx = jnp.where(j == 0, 0, jnp.where(j == nb - 1, 2, 1))
        bias = bias_ref[bidx]
        win = pl.ds(ks, KW)
        scs = {}
        for s in range(ATT_QWIDTH // LANES):
            hk = (2 * s) // (ATT_Q_HEADS // ATT_KV_HEADS)
            qs = aq_ref[0, rows, s * LANES:(s + 1) * LANES]
            for par in range(2):
                scs[s, par] = lax.dot_general(qs, kvar[2 * hk + par, win, :], nt,
                                              preferred_element_type=F32) + bias
        return win, scs

    def attend_finish(jj, win, scs):
        rows = slice(jj * ATT_BLOCK, (jj + 1) * ATT_BLOCK)
        for s in range(ATT_QWIDTH // LANES):
            hk = (2 * s) // (ATT_Q_HEADS // ATT_KV_HEADS)
            acc = jnp.zeros((ATT_BLOCK, 2 * LANES), F32)
            den_sink = []
            for par in range(2):
                sc = scs[s, par]
                sk = sink_ref[0, 2 * s + par] * LOG2E
                mx = jnp.maximum(jnp.max(sc, axis=-1, keepdims=True), sk)
                p = jnp.exp2(sc - mx).astype(BF16)
                acc = acc + jnp.dot(p, vvar[2 * hk + par, win, :], preferred_element_type=F32)
                den_sink.append(jnp.exp2(sk - mx))
            den = acc[:, LANES:] + jnp.where(lo_q, den_sink[0], den_sink[1])
            att_scr[rows, s * LANES:(s + 1) * LANES] = (acc[:, :LANES] / den).astype(BF16)

    def dense(jj):
        rows = slice(jj * ATT_BLOCK, (jj + 1) * ATT_BLOCK)
        y_att = jnp.dot(att_scr[rows, :], watt_ref[...], preferred_element_type=F32)
        merged = (gr_ref[0, rows, :].astype(F32) * yrec_scr[rows, :]
                  + gt_ref[0, rows, :].astype(F32) * y_att)
        z = jnp.dot(merged.astype(BF16), wout_ref[...], preferred_element_type=F32)
        r = DEEPNORM_ALPHA * x_ref[0, rows, :] + (1.0 + ga_ref[0]) * z
        x1 = _layer_norm(r, lg_ref[...], lb_ref[...])
        x1_o[0, rows, :] = x1
        u2_o[0, rows, :] = (x1 * (1.0 + sc_ref[0]) + sh_ref[0]).astype(BF16)

    yrec_scr[...] = jnp.dot(orec_ref[0], wrec_ref[...], preferred_element_type=F32)
    attend_finish(0, *attend_scores(0))
    for jj in range(n_blocks):
        if jj + 1 < n_blocks:
            win, scs = attend_scores(jj + 1)
        dense(jj)
        if jj + 1 < n_blocks:
            attend_finish(jj + 1, win, scs)


def _mix(x, ga1, sc2, sh2, aq, ak, av, orec, g_rec, g_att, sink, w_rec, w_att, w_out, ln_g, ln_b, tq):
    B, S, D = x.shape
    bias = _att_bias(S)
    tok = lambda b, t: (b, t, 0)
    per_b = lambda b, t: (b, 0, 0)
    c2 = lambda b, t: (0, 0)
    full = lambda a: pl.BlockSpec(a.shape, (lambda b, t: (0,) * a.ndim))
    return pl.pallas_call(
        _mix_kernel,
        grid=(B, S // tq),
        in_specs=[pl.BlockSpec((1, tq, D), tok),
                  pl.BlockSpec((1, 1, D), per_b), pl.BlockSpec((1, 1, D), per_b),
                  pl.BlockSpec((1, 1, D), per_b),
                  pl.BlockSpec((1, tq, ATT_QWIDTH), tok),
                  pl.BlockSpec((1, S, ATT_KVWIDTH), per_b),
                  pl.BlockSpec((1, S, ATT_KVWIDTH), per_b),
                  pl.BlockSpec((1, tq, REC_WIDTH), tok),
                  pl.BlockSpec((1, tq, D), tok), pl.BlockSpec((1, tq, D), tok),
                  pl.BlockSpec(memory_space=pltpu.SMEM),
                  full(bias), full(w_rec), full(w_att), full(w_out),
                  pl.BlockSpec((1, D), c2), pl.BlockSpec((1, D), c2)],
        out_specs=[pl.BlockSpec((1, tq, D), tok), pl.BlockSpec((1, tq, D), tok)],
        out_shape=[jax.ShapeDtypeStruct((B, S, D), F32), jax.ShapeDtypeStruct((B, S, D), BF16)],
        scratch_shapes=[pltpu.VMEM((4, S, LANES), BF16),
                        pltpu.VMEM((4, S, 2 * LANES), BF16),
                        pltpu.VMEM((tq, ATT_QWIDTH), BF16),
                        pltpu.VMEM((tq, D), F32)],
        compiler_params=pltpu.CompilerParams(
            dimension_semantics=("arbitrary", "arbitrary"), vmem_limit_bytes=VMEM_LIMIT),
        name="mix",
    )(x, ga1, sc2, sh2, aq, ak, av, orec, g_rec, g_att, sink.reshape(1, ATT_Q_HEADS), bias,
      w_rec, w_att, w_out, ln_g.reshape(1, D), ln_b.reshape(1, D))


HALO = 16
FFN_CHUNK = 256
FFN_ROWS = 128


def _gelu_tanh(x):
    return 0.5 * x * (1.0 + jnp.tanh(np.sqrt(2.0 / np.pi).astype(np.float32) * (x + 0.044715 * (x * x * x))))


def _ffn_kernel(u_ref, up_ref, un_ref, x1_ref, ga_ref, wup_ref, cw_ref, cb_ref, wdn_ref,
                lg_ref, lb_ref, o_ref, ucat, h_scr, acc_ref):
    t = pl.program_id(1)
    nt_ = pl.num_programs(1)
    tm = u_ref.shape[1]
    d_ff = wdn_ref.shape[0]
    ucat[0:HALO] = jnp.where(t > 0, up_ref[0], jnp.zeros_like(up_ref[0]))
    ucat[HALO:HALO + tm] = u_ref[0]
    ucat[HALO + tm:] = jnp.where(t < nt_ - 1, un_ref[0], jnp.zeros_like(un_ref[0]))
    u = ucat[...]

    n_chunks = d_ff // FFN_CHUNK

    def up(j, part):
        if j < n_chunks:
            col = part * d_ff + j * FFN_CHUNK
            h = jnp.dot(u, wup_ref[:, col:col + FFN_CHUNK], preferred_element_type=F32)
            for s in range(FFN_CHUNK // LANES):
                h_scr[j % 2, part, s] = h[:, s * LANES:(s + 1) * LANES]

    def conv(j, part, r0):
        slabs = []
        for s in range(FFN_CHUNK // LANES):
            col = part * d_ff + j * FFN_CHUNK + s * LANES
            w = cw_ref[:, col:col + LANES]
            out = cb_ref[:, col:col + LANES]
            for tap in range(CONV_K):
                lo = HALO + tap - CONV_K // 2 + r0
                out = out + h_scr[j % 2, part, s, lo:lo + FFN_ROWS, :] * w[tap:tap + 1]
            slabs.append(out)
        return jnp.concatenate(slabs, axis=1)

    def down(j, r0):
        rows = slice(r0, r0 + FFN_ROWS)
        a = (_gelu_tanh(conv(j, 1, r0)) * conv(j, 0, r0)).astype(BF16)
        part = jnp.dot(a, wdn_ref[j * FFN_CHUNK:(j + 1) * FFN_CHUNK, :], preferred_element_type=F32)
        if j == 0:
            acc_ref[rows, :] = part
        elif j < n_chunks - 1:
            acc_ref[rows, :] += part
        else:
            r = DEEPNORM_ALPHA * x1_ref[0, rows, :] + (1.0 + ga_ref[0]) * (acc_ref[rows, :] + part)
            o_ref[0, rows, :] = _layer_norm(r, lg_ref[...], lb_ref[...])

    up(0, 0)
    up(0, 1)
    row_blocks = list(range(0, tm, FFN_ROWS))
    half = len(row_blocks) // 2
    for j in range(n_chunks):
        up(j + 1, 0)
        for r0 in row_blocks[:half]:
            down(j, r0)
        up(j + 1, 1)
        for r0 in row_blocks[half:]:
            down(j, r0)


def _ffn(u2, x1, ga2, w_up, conv_w, conv_b, w_down, ln_g, ln_b, tm):
    B, S, D = x1.shape
    d_ff = w_down.shape[0]
    assert d_ff % FFN_CHUNK == 0 and tm % HALO == 0
    nh = tm // HALO
    last = S // HALO - 1
    tok = lambda b, t: (b, t, 0)
    per_b = lambda b, t: (b, 0, 0)
    c2 = lambda b, t: (0, 0)
    return pl.pallas_call(
        _ffn_kernel,
        grid=(B, S // tm),
        in_specs=[pl.BlockSpec((1, tm, D), tok),
                  pl.BlockSpec((1, HALO, D), lambda b, t: (b, jnp.maximum(t * nh - 1, 0), 0)),
                  pl.BlockSpec((1, HALO, D), lambda b, t: (b, jnp.minimum((t + 1) * nh, last), 0)),
                  pl.BlockSpec((1, tm, D), tok),
                  pl.BlockSpec((1, 1, D), per_b),
                  pl.BlockSpec(w_up.shape, c2, pipeline_mode=pl.Buffered(1)),
                  pl.BlockSpec(conv_w.shape, c2),
                  pl.BlockSpec((1, 2 * d_ff), c2),
                  pl.BlockSpec(w_down.shape, c2, pipeline_mode=pl.Buffered(1)),
                  pl.BlockSpec((1, D), c2), pl.BlockSpec((1, D), c2)],
        out_specs=pl.BlockSpec((1, tm, D), tok),
        out_shape=jax.ShapeDtypeStruct((B, S, D), F32),
        scratch_shapes=[pltpu.VMEM((tm + 2 * HALO, D), BF16),
                        pltpu.VMEM((2, 2, FFN_CHUNK // LANES, tm + 2 * HALO, LANES), F32),
                        pltpu.VMEM((tm, D), F32)],
        compiler_params=pltpu.CompilerParams(vmem_limit_bytes=VMEM_LIMIT),
        name="ffn",
    )(u2, u2, u2, x1, ga2, w_up, conv_w, conv_b.reshape(1, 2 * d_ff), w_down,
      ln_g.reshape(1, D), ln_b.reshape(1, D))


def kernel(x, c, positions, w_ada, b_ada, w_in, rec_lower_bound, rec_norm_g, w_rec_branch,
           attn_sink, w_attn_branch, w_out, ln1_g, ln1_b, w_up, conv_w, conv_b, w_down,
           ln2_g, ln2_b):
    B, S, D = x.shape
    assert w_ada.shape[0] == DEPTH and rec_lower_bound.shape[1] == DEPTH + 1
    tm_proj = min(512, S)
    tq = min(512, S)
    tm_ffn = min(512, S)
    for l in range(DEPTH):
        mods = _ada(c, w_ada[l], b_ada[l])
        sh1, sc1, ga1, sh2, sc2, ga2 = [m.reshape(B, 1, D) for m in jnp.split(mods, 6, axis=-1)]
        (q, kf, kb, lfh, lfl, lbh, lbl, v, g, aq, ak, av, g_rec, g_att) = _proj(
            x, sc1, sh1, positions, w_in[l].astype(BF16), rec_lower_bound, tm_proj)
        orec = _rec(q, kf, kb, lfh, lfl, lbh, lbl, v, g, rec_norm_g[l])
        x1, u2 = _mix(x, ga1, sc2, sh2, aq, ak, av, orec, g_rec, g_att, attn_sink[l],
                      w_rec_branch[l].astype(BF16), w_attn_branch[l].astype(BF16),
                      w_out[l].astype(BF16), ln1_g[l], ln1_b[l], tq)
        x = _ffn(u2, x1, ga2, w_up[l].astype(BF16), conv_w[l], conv_b[l], w_down[l].astype(BF16),
                 ln2_g[l], ln2_b[l], tm_ffn)
    return x
```

```python
import numpy as np
import jax
import jax.numpy as jnp
from jax import lax
from jax.experimental import pallas as pl
from jax.experimental.pallas import tpu as pltpu

F32 = jnp.float32
BF16 = jnp.bfloat16

REC_HEADS = 4
REC_DK = 128
REC_DV = 128
REC_WIDTH = REC_HEADS * REC_DK
ATT_Q_HEADS = 8
ATT_KV_HEADS = 2
ATT_HEAD_DIM = 64
ATT_QWIDTH = ATT_Q_HEADS * ATT_HEAD_DIM
ATT_KVWIDTH = ATT_KV_HEADS * ATT_HEAD_DIM
ATT_WINDOW = 128
ATT_BLOCK = 128
ROPE_THETA = 10000.0
CONV_K = 3
LN_EPS = 1e-5
RMS_EPS = 1e-6
DEPTH = 1
DEEPNORM_ALPHA = (2 * DEPTH) ** 0.25

LANES = 128
REC_CHUNK = 64
REC_LEVELS = (32, 16, 8, 4, 2, 1)
LOG2E = 1.4426950408889634
NEG_BIG = -1e30
VMEM_LIMIT = 56 * 1024 * 1024


def _sigmoid(x):
    return 1.0 / (1.0 + jnp.exp(-x))


def _silu(x):
    return x * _sigmoid(x)


def _layer_norm(r, g, b):
    mu = jnp.mean(r, axis=-1, keepdims=True)
    rc = r - mu
    var = jnp.mean(rc * rc, axis=-1, keepdims=True)
    return rc * lax.rsqrt(var + LN_EPS) * g + b


def _ada_kernel(c_ref, w_ref, b_ref, o_ref):
    c = c_ref[...]
    o_ref[...] = jnp.dot(_silu(c), w_ref[...], preferred_element_type=F32) + b_ref[...]


def _ada(c, w_ada, b_ada):
    B, D = c.shape
    N = w_ada.shape[1]
    return pl.pallas_call(
        _ada_kernel,
        grid=(N // D,),
        in_specs=[pl.BlockSpec((B, D), lambda j: (0, 0)),
                  pl.BlockSpec((D, D), lambda j: (0, j)),
                  pl.BlockSpec((1, D), lambda j: (0, j))],
        out_specs=pl.BlockSpec((B, D), lambda j: (0, j)),
        out_shape=jax.ShapeDtypeStruct((B, N), F32),
        name="ada",
    )(c, w_ada, b_ada.reshape(1, N))


PROJ_COLS = 256


def _proj_kernel(x_ref, sc_ref, sh_ref, pos_ref, invf_ref, w_ref, rlb_ref,
                 q_o, kf_o, kb_o, lf_o, lb_o, v_o, g_o, aq_o, ak_o, av_o, gr_o, ga_o):
    u = (x_ref[0] * (1.0 + sc_ref[0]) + sh_ref[0]).astype(BF16)
    D = x_ref.shape[-1]
    W = REC_WIDTH

    def pieces(base, width):
        for c in range(0, width, PROJ_COLS):
            yield slice(c, c + PROJ_COLS), jnp.dot(
                u, w_ref[:, base + c:base + c + PROJ_COLS], preferred_element_type=F32)

    for sl, y in pieces(0, W):
        q_o[0, :, sl] = (_silu(y) * (REC_DK ** -0.5)).astype(BF16)

    for direction, (k_o, l_o) in enumerate(((kf_o, lf_o), (kb_o, lb_o))):
        r = rlb_ref[direction]
        e = jnp.exp(r - jnp.max(r, axis=0, keepdims=True))
        lbd = e[0:1] / jnp.sum(e, axis=0, keepdims=True)
        for sl, y in pieces((1 + direction) * W, W):
            f = lbd[:, sl] + (1.0 - lbd[:, sl]) * _sigmoid(y)
            k_o[0, :, sl] = (1.0 - f).astype(BF16)
            l_o[0, :, sl] = jnp.log(f)

    for sl, y in pieces(3 * W, W):
        v_o[0, :, sl] = y.astype(BF16)
    for sl, y in pieces(4 * W, W):
        g_o[0, :, sl] = _silu(y).astype(BF16)

    ang = pos_ref[0].astype(F32) * invf_ref[...]
    cos = jnp.cos(ang)
    sin = jnp.sin(ang)
    lane = lax.broadcasted_iota(jnp.int32, ang.shape, 1)
    first_half = (lane % ATT_HEAD_DIM) < (ATT_HEAD_DIM // 2)
    sin_signed = jnp.where(first_half, -sin, sin)

    def rope(t):
        partner = jnp.where(first_half,
                            pltpu.roll(t, LANES - ATT_HEAD_DIM // 2, axis=1),
                            pltpu.roll(t, ATT_HEAD_DIM // 2, axis=1))
        return t * cos + partner * sin_signed

    c0 = 5 * W
    for sl, y in pieces(c0, ATT_QWIDTH):
        for s in range(PROJ_COLS // LANES):
            lanes = slice(sl.start + s * LANES, sl.start + (s + 1) * LANES)
            aq_o[0, :, lanes] = (rope(y[:, s * LANES:(s + 1) * LANES])
                                 * (ATT_HEAD_DIM ** -0.5 * LOG2E)).astype(BF16)
    c0 += ATT_QWIDTH
    assert 2 * ATT_KVWIDTH == PROJ_COLS
    for _, y in pieces(c0, 2 * ATT_KVWIDTH):
        ak_o[0] = rope(y[:, :ATT_KVWIDTH]).astype(BF16)
        av_o[0] = y[:, ATT_KVWIDTH:].astype(BF16)
    c0 += 2 * ATT_KVWIDTH
    for sl, y in pieces(c0, D):
        gr_o[0, :, sl] = _sigmoid(y).astype(BF16)
    for sl, y in pieces(c0 + D, D):
        ga_o[0, :, sl] = _sigmoid(y).astype(BF16)


def _proj(x, sc1, sh1, positions, w_in_bf, rec_lower_bound, tm):
    B, S, D = x.shape
    d_in = w_in_bf.shape[1]
    half = ATT_HEAD_DIM // 2
    inv_freq = ROPE_THETA ** (-jnp.arange(half, dtype=F32) / half)
    invf = jnp.tile(inv_freq, LANES // half).reshape(1, LANES)
    W = REC_WIDTH
    tok = lambda b, t: (b, t, 0)
    per_b = lambda b, t: (b, 0, 0)
    outs = [(W, BF16), (W, BF16), (W, BF16), (W, F32), (W, F32), (W, BF16), (W, BF16),
            (ATT_QWIDTH, BF16), (ATT_KVWIDTH, BF16), (ATT_KVWIDTH, BF16), (D, BF16), (D, BF16)]
    return pl.pallas_call(
        _proj_kernel,
        grid=(B, S // tm),
        in_specs=[pl.BlockSpec((1, tm, D), tok),
                  pl.BlockSpec((1, 1, D), per_b),
                  pl.BlockSpec((1, 1, D), per_b),
                  pl.BlockSpec((1, tm, 1), tok),
                  pl.BlockSpec((1, LANES), lambda b, t: (0, 0)),
                  pl.BlockSpec((D, d_in), lambda b, t: (0, 0), pipeline_mode=pl.Buffered(1)),
                  pl.BlockSpec(rec_lower_bound.shape, lambda b, t: (0, 0, 0))],
        out_specs=[pl.BlockSpec((1, tm, w), tok) for w, _ in outs],
        out_shape=[jax.ShapeDtypeStruct((B, S, w), dt) for w, dt in outs],
        compiler_params=pltpu.CompilerParams(vmem_limit_bytes=VMEM_LIMIT),
        name="proj",
    )(x, sc1, sh1, positions.reshape(B, S, 1), invf, w_in_bf, rec_lower_bound)


def _rec_constants():
    C = REC_CHUNK
    nblk = 2 + len(REC_LEVELS)
    mf = np.zeros((nblk, C, C), np.float32)
    r = np.arange(C)
    for t in range(C):
        mf[0, t] = r <= t
        mf[1, t] = r > t
        for i, m in enumerate(REC_LEVELS):
            boundary = (t // (2 * m)) * 2 * m + m
            if t >= boundary:
                mf[2 + i, t] = (r >= boundary) & (r <= t)
            else:
                mf[2 + i, t] = (r > t) & (r <= boundary - 1)
    mb = mf[:, ::-1, ::-1]
    x = r[:, None] ^ r[None, :]
    lev = np.where(x > 0, np.floor(np.log2(np.maximum(x, 1))).astype(np.int32), -1)
    invalid = -2
    levf = np.where(r[:, None] >= r[None, :], lev, invalid).astype(np.int32)
    levb = np.where(r[:, None] <= r[None, :], lev, invalid).astype(np.int32)
    m3 = lambda m: np.concatenate([m.reshape(nblk * C, C)] * 3, axis=1)
    return (jnp.asarray(np.stack([m3(mf), m3(mb)]), BF16),
            jnp.asarray(np.stack([levf, levb])))


REC_HEADS_PER_STEP = 2
REC_UNROLL = 2
REC_STATE_UNROLL = 16


def _rec_kernel(q_ref, kf_ref, kb_ref, lf_ref, lb_ref, v_ref, g_ref, ng_ref, m_ref, lev_ref,
                o_ref, acc_ref, qt_ref, upd_ref, dec_ref):
    C = REC_CHUNK
    HP = REC_HEADS_PER_STEP
    S = q_ref.shape[1]
    n_chunks = S // C
    row = lax.broadcasted_iota(jnp.int32, (C, LANES), 0)
    nt = (((1,), (1,)), ((), ()))
    tn = (((0,), (0,)), ((), ()))
    k_refs = (kf_ref, kb_ref)
    l_refs = (lf_ref, lb_ref)

    def local(i, carry):
        U = REC_UNROLL
        units = [(c, d, h) for c in range(U) for d in range(2) for h in range(HP)]
        ns = {(c, d): (U * i + c if d == 0 else n_chunks - 1 - (U * i + c))
              for c in range(U) for d in range(2)}
        rows = {cd: pl.ds(pl.multiple_of(n * C, C), C) for cd, n in ns.items()}
        sls = [slice(h * LANES, (h + 1) * LANES) for h in range(HP)]
        xs = {}
        for cd in ns:
            lg = l_refs[cd[1]][0, rows[cd], :]
            hi = lg.astype(BF16)
            r1 = lg - hi.astype(F32)
            mid = r1.astype(BF16)
            lo = (r1 - mid.astype(F32)).astype(BF16)
            e = jnp.dot(m_ref[cd[1]], jnp.concatenate([hi, mid, lo], axis=0),
                        preferred_element_type=F32)
            xs[cd] = jnp.exp(e)
        ws, kts, qs, ks, vs = {}, {}, {}, {}, {}
        for c, d in ns:
            cd = (c, d)
            last = C - 1 if d == 0 else 0
            dec_ref[d, pl.ds(ns[cd], 1), :] = xs[cd][last:last + 1]
            for h in range(HP):
                un = (c, d, h)
                q = q_ref[0, rows[cd], sls[h]]
                k = k_refs[d][0, rows[cd], sls[h]]
                qs[un], ks[un], vs[un] = q, k, v_ref[0, rows[cd], sls[h]]
                qf = q.astype(F32)
                kf = k.astype(F32)
                qt_ref[d, rows[cd], sls[h]] = (qf * xs[cd][0:C, sls[h]]).astype(BF16)
                kts[un] = (kf * xs[cd][C:2 * C, sls[h]]).astype(BF16)
                for j, m in enumerate(REC_LEVELS):
                    q_side = ((row & m) != 0) if d == 0 else ((row & m) == 0)
                    ws[un, j] = (jnp.where(q_side, qf, kf)
                                 * xs[cd][(2 + j) * C:(3 + j) * C, sls[h]]).astype(BF16)
        als = {}
        for un in units:
            als[un, -1] = lax.dot_general(qs[un], ks[un], nt, preferred_element_type=F32)
            for j in range(len(REC_LEVELS)):
                als[un, j] = lax.dot_general(ws[un, j], ws[un, j], nt, preferred_element_type=F32)
        for un in units:
            c, d, h = un
            upd_ref[d, h, ns[c, d]] = lax.dot_general(vs[un], kts[un], tn,
                                                      preferred_element_type=F32)
        avs = {}
        for un in units:
            lev = lev_ref[un[1]]
            a = jnp.where(lev == -1, als[un, -1], 0.0)
            for j, m in enumerate(REC_LEVELS):
                a = jnp.where(lev == (m.bit_length() - 1), als[un, j], a)
            avs[un] = a.astype(BF16)
        for un in units:
            c, d, h = un
            acc_ref[d, rows[c, d], sls[h]] = jnp.dot(avs[un], vs[un], preferred_element_type=F32)
        return carry

    lax.fori_loop(0, n_chunks // REC_UNROLL, local, 0)

    for direction in range(2):
        def carried(i, sts, direction=direction):
            n = i if direction == 0 else n_chunks - 1 - i
            rows = pl.ds(pl.multiple_of(n * C, C), C)
            d = dec_ref[direction, pl.ds(n, 1), :]
            out = []
            for h in range(HP):
                sl = slice(h * LANES, (h + 1) * LANES)
                st = sts[h]
                acc_ref[direction, rows, sl] += lax.dot_general(
                    qt_ref[direction, rows, sl], st.astype(BF16), nt, preferred_element_type=F32)
                out.append(st * d[:, sl] + upd_ref[direction, h, n])
            return tuple(out)

        lax.fori_loop(0, n_chunks, carried,
                      tuple(jnp.zeros((REC_DV, REC_DK), F32) for _ in range(HP)),
                      unroll=REC_STATE_UNROLL)

    RB = 256

    def finish(i, carry):
        rows = pl.ds(pl.multiple_of(i * RB, RB), RB)
        for h in range(HP):
            sl = slice(h * LANES, (h + 1) * LANES)
            o = acc_ref[0, rows, sl] + acc_ref[1, rows, sl]
            o = o * lax.rsqrt(jnp.mean(o * o, axis=-1, keepdims=True) + RMS_EPS) * ng_ref[...]
            o_ref[0, rows, sl] = (o * g_ref[0, rows, sl].astype(F32)).astype(BF16)
        return carry

    lax.fori_loop(0, S // RB, finish, 0, unroll=2)


def _rec(q, kf, kb, lf, lb, v, g, norm_g):
    B, S, W = q.shape
    HP = REC_HEADS_PER_STEP
    m3, lev = _rec_constants()
    n_chunks = S // REC_CHUNK
    head = lambda b, h: (b, 0, h)
    blk = pl.BlockSpec((1, S, HP * LANES), head)
    return pl.pallas_call(
        _rec_kernel,
        grid=(B, REC_HEADS // HP),
        in_specs=[blk, blk, blk, blk, blk, blk, blk,
                  pl.BlockSpec((1, REC_DV), lambda b, h: (0, 0)),
                  pl.BlockSpec(m3.shape, lambda b, h: (0, 0, 0)),
                  pl.BlockSpec(lev.shape, lambda b, h: (0, 0, 0))],
        out_specs=blk,
        out_shape=jax.ShapeDtypeStruct((B, S, W), BF16),
        scratch_shapes=[pltpu.VMEM((2, S, HP * REC_DV), F32),
                        pltpu.VMEM((2, S, HP * REC_DK), BF16),
                        pltpu.VMEM((2, HP, n_chunks, REC_DV, REC_DK), F32),
                        pltpu.VMEM((2, n_chunks, HP * REC_DK), F32)],
        compiler_params=pltpu.CompilerParams(vmem_limit_bytes=VMEM_LIMIT),
        name="rec",
    )(q, kf, kb, lf, lb, v, g, norm_g.reshape(1, REC_DV), m3, lev)


def _att_bias(S):
    rowi = np.arange(ATT_BLOCK)[:, None]
    coli = np.arange(3 * ATT_BLOCK)[None, :]
    out = []
    for off in (0, ATT_BLOCK, 2 * ATT_BLOCK):
        ok = np.abs(coli - off - rowi) <= ATT_WINDOW
        out.append(np.where(ok, 0.0, NEG_BIG))
    return jnp.asarray(np.stack(out), F32)


def _mix_kernel(x_ref, ga_ref, sc_ref, sh_ref, aq_ref, ak_ref, av_ref, orec_ref, gr_ref, gt_ref,
                sink_ref, bias_ref, wrec_ref, watt_ref, wout_ref, lg_ref, lb_ref,
                x1_o, u2_o, kvar, vvar, att_scr, yrec_scr):
    t = pl.program_id(1)
    S = ak_ref.shape[1]
    TQ = x_ref.shape[1]
    nb = S // ATT_BLOCK
    KW = 3 * ATT_BLOCK
    HALF = LANES // 2
    nt = (((1,), (1,)), ((), ()))

    @pl.when(t == 0)
    def _():
        lane = lax.broadcasted_iota(jnp.int32, (S, LANES), 1)
        lo = lane < HALF
        k = ak_ref[0].astype(F32)
        v = av_ref[0].astype(F32)
        ksw = pltpu.roll(k, HALF, axis=1)
        vsw = pltpu.roll(v, HALF, axis=1)
        zero = jnp.zeros_like(k)
        for i, (kk, vv) in enumerate([(jnp.where(lo, k, zero), jnp.where(lo, v, zero)),
                                      (jnp.where(lo, zero, ksw), jnp.where(lo, zero, vsw)),
                                      (jnp.where(lo, ksw, zero), jnp.where(lo, vsw, zero)),
                                      (jnp.where(lo, zero, k), jnp.where(lo, zero, v))]):
            kvar[i] = kk.astype(BF16)
            vvar[i, :, 0:LANES] = vv.astype(BF16)
            ones = jnp.where(lo, 1.0, 0.0) if i % 2 == 0 else jnp.where(lo, 0.0, 1.0)
            vvar[i, :, LANES:2 * LANES] = ones.astype(BF16)

    lane_q = lax.broadcasted_iota(jnp.int32, (ATT_BLOCK, LANES), 1)
    lo_q = lane_q < HALF
    n_blocks = TQ // ATT_BLOCK

    def attend_scores(jj):
        rows = slice(jj * ATT_BLOCK, (jj + 1) * ATT_BLOCK)
        j = t * n_blocks + jj
        ks = pl.multiple_of(jnp.clip((j - 1) * ATT_BLOCK, 0, S - KW), ATT_BLOCK)
        bidx = jnp.where(j == 0, 0, jnp.where(j == nb - 1, 2, 1))
        bias = bias_ref[bidx]
        win = pl.ds(ks, KW)
        scs = {}
        for s in range(ATT_QWIDTH // LANES):
            hk = (2 * s) // (ATT_Q_HEADS // ATT_KV_HEADS)
            qs = aq_ref[0, rows, s * LANES:(s + 1) * LANES]
            for par in range(2):
                scs[s, par] = lax.dot_general(qs, kvar[2 * hk + par, win, :], nt,
                                              preferred_element_type=F32) + bias
        return win, scs

    def attend_finish(jj, win, scs):
        rows = slice(jj * ATT_BLOCK, (jj + 1) * ATT_BLOCK)
        for s in range(ATT_QWIDTH // LANES):
            hk = (2 * s) // (ATT_Q_HEADS // ATT_KV_HEADS)
            acc = jnp.zeros((ATT_BLOCK, 2 * LANES), F32)
            den_sink = []
            for par in range(2):
                sc = scs[s, par]
                sk = sink_ref[0, 2 * s + par] * LOG2E
                mx = jnp.maximum(jnp.max(sc, axis=-1, keepdims=True), sk)
                p = jnp.exp2(sc - mx).astype(BF16)
                acc = acc + jnp.dot(p, vvar[2 * hk + par, win, :], preferred_element_type=F32)
                den_sink.append(jnp.exp2(sk - mx))
            den = acc[:, LANES:] + jnp.where(lo_q, den_sink[0], den_sink[1])
            att_scr[rows, s * LANES:(s + 1) * LANES] = (acc[:, :LANES] / den).astype(BF16)

    def dense(jj):
        rows = slice(jj * ATT_BLOCK, (jj + 1) * ATT_BLOCK)
        y_att = jnp.dot(att_scr[rows, :], watt_ref[...], preferred_element_type=F32)
        merged = (gr_ref[0, rows, :].astype(F32) * yrec_scr[rows, :]
                  + gt_ref[0, rows, :].astype(F32) * y_att)
        z = jnp.dot(merged.astype(BF16), wout_ref[...], preferred_element_type=F32)
        r = DEEPNORM_ALPHA * x_ref[0, rows, :] + (1.0 + ga_ref[0]) * z
        x1 = _layer_norm(r, lg_ref[...], lb_ref[...])
        x1_o[0, rows, :] = x1
        u2_o[0, rows, :] = (x1 * (1.0 + sc_ref[0]) + sh_ref[0]).astype(BF16)

    yrec_scr[...] = jnp.dot(orec_ref[0], wrec_ref[...], preferred_element_type=F32)
    attend_finish(0, *attend_scores(0))
    for jj in range(n_blocks):
        if jj + 1 < n_blocks:
            win, scs = attend_scores(jj + 1)
        dense(jj)
        if jj + 1 < n_blocks:
            attend_finish(jj + 1, win, scs)


def _mix(x, ga1, sc2, sh2, aq, ak, av, orec, g_rec, g_att, sink, w_rec, w_att, w_out, ln_g, ln_b, tq):
    B, S, D = x.shape
    bias = _att_bias(S)
    tok = lambda b, t: (b, t, 0)
    per_b = lambda b, t: (b, 0, 0)
    c2 = lambda b, t: (0, 0)
    full = lambda a: pl.BlockSpec(a.shape, (lambda b, t: (0,) * a.ndim))
    return pl.pallas_call(
        _mix_kernel,
        grid=(B, S // tq),
        in_specs=[pl.BlockSpec((1, tq, D), tok),
                  pl.BlockSpec((1, 1, D), per_b), pl.BlockSpec((1, 1, D), per_b),
                  pl.BlockSpec((1, 1, D), per_b),
                  pl.BlockSpec((1, tq, ATT_QWIDTH), tok),
                  pl.BlockSpec((1, S, ATT_KVWIDTH), per_b),
                  pl.BlockSpec((1, S, ATT_KVWIDTH), per_b),
                  pl.BlockSpec((1, tq, REC_WIDTH), tok),
                  pl.BlockSpec((1, tq, D), tok), pl.BlockSpec((1, tq, D), tok),
                  pl.BlockSpec(memory_space=pltpu.SMEM),
                  full(bias), full(w_rec), full(w_att), full(w_out),
                  pl.BlockSpec((1, D), c2), pl.BlockSpec((1, D), c2)],
        out_specs=[pl.BlockSpec((1, tq, D), tok), pl.BlockSpec((1, tq, D), tok)],
        out_shape=[jax.ShapeDtypeStruct((B, S, D), F32), jax.ShapeDtypeStruct((B, S, D), BF16)],
        scratch_shapes=[pltpu.VMEM((4, S, LANES), BF16),
                        pltpu.VMEM((4, S, 2 * LANES), BF16),
                        pltpu.VMEM((tq, ATT_QWIDTH), BF16),
                        pltpu.VMEM((tq, D), F32)],
        compiler_params=pltpu.CompilerParams(
            dimension_semantics=("arbitrary", "arbitrary"), vmem_limit_bytes=VMEM_LIMIT),
        name="mix",
    )(x, ga1, sc2, sh2, aq, ak, av, orec, g_rec, g_att, sink.reshape(1, ATT_Q_HEADS), bias,
      w_rec, w_att, w_out, ln_g.reshape(1, D), ln_b.reshape(1, D))


HALO = 16
FFN_CHUNK = 256
FFN_ROWS = 128


def _gelu_tanh(x):
    return 0.5 * x * (1.0 + jnp.tanh(np.sqrt(2.0 / np.pi).astype(np.float32) * (x + 0.044715 * (x * x * x))))


def _ffn_kernel(u_ref, up_ref, un_ref, x1_ref, ga_ref, wup_ref, cw_ref, cb_ref, wdn_ref,
                lg_ref, lb_ref, o_ref, ucat, h_scr, acc_ref):
    t = pl.program_id(1)
    nt_ = pl.num_programs(1)
    tm = u_ref.shape[1]
    d_ff = wdn_ref.shape[0]
    ucat[0:HALO] = jnp.where(t > 0, up_ref[0], jnp.zeros_like(up_ref[0]))
    ucat[HALO:HALO + tm] = u_ref[0]
    ucat[HALO + tm:] = jnp.where(t < nt_ - 1, un_ref[0], jnp.zeros_like(un_ref[0]))
    u = ucat[...]

    n_chunks = d_ff // FFN_CHUNK

    def up(j, part):
        if j < n_chunks:
            col = part * d_ff + j * FFN_CHUNK
            h = jnp.dot(u, wup_ref[:, col:col + FFN_CHUNK], preferred_element_type=F32)
            for s in range(FFN_CHUNK // LANES):
                h_scr[j % 2, part, s] = h[:, s * LANES:(s + 1) * LANES]

    def conv(j, part, r0):
        slabs = []
        for s in range(FFN_CHUNK // LANES):
            col = part * d_ff + j * FFN_CHUNK + s * LANES
            w = cw_ref[:, col:col + LANES]
            out = cb_ref[:, col:col + LANES]
            for tap in range(CONV_K):
                lo = HALO + tap - CONV_K // 2 + r0
                out = out + h_scr[j % 2, part, s, lo:lo + FFN_ROWS, :] * w[tap:tap + 1]
            slabs.append(out)
        return jnp.concatenate(slabs, axis=1)

    def down(j, r0):
        rows = slice(r0, r0 + FFN_ROWS)
        a = (_gelu_tanh(conv(j, 1, r0)) * conv(j, 0, r0)).astype(BF16)
        part = jnp.dot(a, wdn_ref[j * FFN_CHUNK:(j + 1) * FFN_CHUNK, :], preferred_element_type=F32)
        if j == 0:
            acc_ref[rows, :] = part
        elif j < n_chunks - 1:
            acc_ref[rows, :] += part
        else:
            r = DEEPNORM_ALPHA * x1_ref[0, rows, :] + (1.0 + ga_ref[0]) * (acc_ref[rows, :] + part)
            o_ref[0, rows, :] = _layer_norm(r, lg_ref[...], lb_ref[...])

    up(0, 0)
    up(0, 1)
    row_blocks = list(range(0, tm, FFN_ROWS))
    half = len(row_blocks) // 2
    for j in range(n_chunks):
        up(j + 1, 0)
        for r0 in row_blocks[:half]:
            down(j, r0)
        up(j + 1, 1)
        for r0 in row_blocks[half:]:
            down(j, r0)


def _ffn(u2, x1, ga2, w_up, conv_w, conv_b, w_down, ln_g, ln_b, tm):
    B, S, D = x1.shape
    d_ff = w_down.shape[0]
    assert d_ff % FFN_CHUNK == 0 and tm % HALO == 0
    nh = tm // HALO
    last = S // HALO - 1
    tok = lambda b, t: (b, t, 0)
    per_b = lambda b, t: (b, 0, 0)
    c2 = lambda b, t: (0, 0)
    return pl.pallas_call(
        _ffn_kernel,
        grid=(B, S // tm),
        in_specs=[pl.BlockSpec((1, tm, D), tok),
                  pl.BlockSpec((1, HALO, D), lambda b, t: (b, jnp.maximum(t * nh - 1, 0), 0)),
                  pl.BlockSpec((1, HALO, D), lambda b, t: (b, jnp.minimum((t + 1) * nh, last), 0)),
                  pl.BlockSpec((1, tm, D), tok),
                  pl.BlockSpec((1, 1, D), per_b),
                  pl.BlockSpec(w_up.shape, c2, pipeline_mode=pl.Buffered(1)),
                  pl.BlockSpec(conv_w.shape, c2),
                  pl.BlockSpec((1, 2 * d_ff), c2),
                  pl.BlockSpec(w_down.shape, c2, pipeline_mode=pl.Buffered(1)),
                  pl.BlockSpec((1, D), c2), pl.BlockSpec((1, D), c2)],
        out_specs=pl.BlockSpec((1, tm, D), tok),
        out_shape=jax.ShapeDtypeStruct((B, S, D), F32),
        scratch_shapes=[pltpu.VMEM((tm + 2 * HALO, D), BF16),
                        pltpu.VMEM((2, 2, FFN_CHUNK // LANES, tm + 2 * HALO, LANES), F32),
                        pltpu.VMEM((tm, D), F32)],
        compiler_params=pltpu.CompilerParams(vmem_limit_bytes=VMEM_LIMIT),
        name="ffn",
    )(u2, u2, u2, x1, ga2, w_up, conv_w, conv_b.reshape(1, 2 * d_ff), w_down,
      ln_g.reshape(1, D), ln_b.reshape(1, D))


def kernel(x, c, positions, w_ada, b_ada, w_in, rec_lower_bound, rec_norm_g, w_rec_branch,
           attn_sink, w_attn_branch, w_out, ln1_g, ln1_b, w_up, conv_w, conv_b, w_down,
           ln2_g, ln2_b):
    B, S, D = x.shape
    assert w_ada.shape[0] == DEPTH and rec_lower_bound.shape[1] == DEPTH + 1
    tm_proj = min(1024, S)
    tq = min(1024, S)
    tm_ffn = min(1024, S)
    for l in range(DEPTH):
        mods = _ada(c, w_ada[l], b_ada[l])
        sh1, sc1, ga1, sh2, sc2, ga2 = [m.reshape(B, 1, D) for m in jnp.split(mods, 6, axis=-1)]
        (q, kf, kb, lf, lb, v, g, aq, ak, av, g_rec, g_att) = _proj(
            x, sc1, sh1, positions, w_in[l].astype(BF16), rec_lower_bound, tm_proj)
        orec = _rec(q, kf, kb, lf, lb, v, g, rec_norm_g[l])
        x1, u2 = _mix(x, ga1, sc2, sh2, aq, ak, av, orec, g_rec, g_att, attn_sink[l],
                      w_rec_branch[l].astype(BF16), w_attn_branch[l].astype(BF16),
                      w_out[l].astype(BF16), ln1_g[l], ln1_b[l], tq)
        x = _ffn(u2, x1, ga2, w_up[l].astype(BF16), conv_w[l], conv_b[l], w_down[l].astype(BF16),
                 ln2_g[l], ln2_b[l], tm_ffn)
    return x
```

```python
import numpy as np
import jax
import jax.numpy as jnp
from jax import lax
from jax.experimental import pallas as pl
from jax.experimental.pallas import tpu as pltpu

F32 = jnp.float32
BF16 = jnp.bfloat16

REC_HEADS = 4
REC_DK = 128
REC_DV = 128
REC_WIDTH = REC_HEADS * REC_DK
ATT_Q_HEADS = 8
ATT_KV_HEADS = 2
ATT_HEAD_DIM = 64
ATT_QWIDTH = ATT_Q_HEADS * ATT_HEAD_DIM
ATT_KVWIDTH = ATT_KV_HEADS * ATT_HEAD_DIM
ATT_WINDOW = 128
ATT_BLOCK = 128
ROPE_THETA = 10000.0
CONV_K = 3
LN_EPS = 1e-5
RMS_EPS = 1e-6
DEPTH = 1
DEEPNORM_ALPHA = (2 * DEPTH) ** 0.25

LANES = 128
REC_CHUNK = 64
REC_LEVELS = (32, 16, 8, 4, 2, 1)
LOG2E = 1.4426950408889634
NEG_BIG = -1e30
VMEM_LIMIT = 56 * 1024 * 1024


def _sigmoid(x):
    return 1.0 / (1.0 + jnp.exp(-x))


def _silu(x):
    return x * _sigmoid(x)


def _layer_norm(r, g, b):
    mu = jnp.mean(r, axis=-1, keepdims=True)
    rc = r - mu
    var = jnp.mean(rc * rc, axis=-1, keepdims=True)
    return rc * lax.rsqrt(var + LN_EPS) * g + b


def _ada_kernel(c_ref, w_ref, b_ref, o_ref):
    c = c_ref[...]
    o_ref[...] = jnp.dot(_silu(c), w_ref[...], preferred_element_type=F32) + b_ref[...]


def _ada(c, w_ada, b_ada):
    B, D = c.shape
    N = w_ada.shape[1]
    return pl.pallas_call(
        _ada_kernel,
        grid=(N // D,),
        in_specs=[pl.BlockSpec((B, D), lambda j: (0, 0)),
                  pl.BlockSpec((D, D), lambda j: (0, j)),
                  pl.BlockSpec((1, D), lambda j: (0, j))],
        out_specs=pl.BlockSpec((B, D), lambda j: (0, j)),
        out_shape=jax.ShapeDtypeStruct((B, N), F32),
        name="ada",
    )(c, w_ada, b_ada.reshape(1, N))


PROJ_COLS = 256


def _proj_kernel(x_ref, sc_ref, sh_ref, pos_ref, invf_ref, w_ref, rlb_ref,
                 q_o, kf_o, kb_o, lf_o, lb_o, v_o, g_o, aq_o, ak_o, av_o, gr_o, ga_o):
    u = (x_ref[0] * (1.0 + sc_ref[0]) + sh_ref[0]).astype(BF16)
    D = x_ref.shape[-1]
    W = REC_WIDTH

    def pieces(base, width):
        for c in range(0, width, PROJ_COLS):
            yield slice(c, c + PROJ_COLS), jnp.dot(
                u, w_ref[:, base + c:base + c + PROJ_COLS], preferred_element_type=F32)

    for sl, y in pieces(0, W):
        q_o[0, :, sl] = (_silu(y) * (REC_DK ** -0.5)).astype(BF16)

    for direction, (k_o, l_o) in enumerate(((kf_o, lf_o), (kb_o, lb_o))):
        r = rlb_ref[direction]
        e = jnp.exp(r - jnp.max(r, axis=0, keepdims=True))
        lbd = e[0:1] / jnp.sum(e, axis=0, keepdims=True)
        for sl, y in pieces((1 + direction) * W, W):
            f = lbd[:, sl] + (1.0 - lbd[:, sl]) * _sigmoid(y)
            k_o[0, :, sl] = (1.0 - f).astype(BF16)
            l_o[0, :, sl] = jnp.log(f)

    for sl, y in pieces(3 * W, W):
        v_o[0, :, sl] = y.astype(BF16)
    for sl, y in pieces(4 * W, W):
        g_o[0, :, sl] = _silu(y).astype(BF16)

    quarter = LANES // 4
    ang_c = pos_ref[0].astype(F32) * invf_ref[...]
    lane = lax.broadcasted_iota(jnp.int32, (x_ref.shape[1], LANES), 1)
    rowm = lax.broadcasted_iota(jnp.int32, (x_ref.shape[1], LANES), 0) % 4

    def expand(tc):
        rep = jnp.repeat(tc, 4, axis=0)
        y = rep
        for j in range(1, 4):
            y = jnp.where(rowm == j, pltpu.roll(rep, LANES - j * quarter, axis=1), y)
        z = jnp.where(lane < quarter, y, 0.0)
        z = z + pltpu.roll(z, quarter, axis=1)
        return z + pltpu.roll(z, 2 * quarter, axis=1)

    cos = expand(jnp.cos(ang_c))
    sin = expand(jnp.sin(ang_c))
    first_half = (lane % ATT_HEAD_DIM) < (ATT_HEAD_DIM // 2)
    sin_signed = jnp.where(first_half, -sin, sin)

    def rope(t):
        partner = jnp.where(first_half,
                            pltpu.roll(t, LANES - ATT_HEAD_DIM // 2, axis=1),
                            pltpu.roll(t, ATT_HEAD_DIM // 2, axis=1))
        return t * cos + partner * sin_signed

    c0 = 5 * W
    for sl, y in pieces(c0, ATT_QWIDTH):
        for s in range(PROJ_COLS // LANES):
            lanes = slice(sl.start + s * LANES, sl.start + (s + 1) * LANES)
            aq_o[0, :, lanes] = (rope(y[:, s * LANES:(s + 1) * LANES])
                                 * (ATT_HEAD_DIM ** -0.5 * LOG2E)).astype(BF16)
    c0 += ATT_QWIDTH
    assert 2 * ATT_KVWIDTH == PROJ_COLS
    for _, y in pieces(c0, 2 * ATT_KVWIDTH):
        ak_o[0] = rope(y[:, :ATT_KVWIDTH]).astype(BF16)
        av_o[0] = y[:, ATT_KVWIDTH:].astype(BF16)
    c0 += 2 * ATT_KVWIDTH
    for sl, y in pieces(c0, D):
        gr_o[0, :, sl] = _sigmoid(y).astype(BF16)
    for sl, y in pieces(c0 + D, D):
        ga_o[0, :, sl] = _sigmoid(y).astype(BF16)


def _proj(x, sc1, sh1, positions, w_in_bf, rec_lower_bound, tm):
    B, S, D = x.shape
    d_in = w_in_bf.shape[1]
    half = ATT_HEAD_DIM // 2
    inv_freq = ROPE_THETA ** (-jnp.arange(half, dtype=F32) / half)
    invf = jnp.tile(inv_freq, LANES // half).reshape(1, LANES)
    W = REC_WIDTH
    tok = lambda b, t: (b, t, 0)
    per_b = lambda b, t: (b, 0, 0)
    outs = [(W, BF16), (W, BF16), (W, BF16), (W, F32), (W, F32), (W, BF16), (W, BF16),
            (ATT_QWIDTH, BF16), (ATT_KVWIDTH, BF16), (ATT_KVWIDTH, BF16), (D, BF16), (D, BF16)]
    return pl.pallas_call(
        _proj_kernel,
        grid=(B, S // tm),
        in_specs=[pl.BlockSpec((1, tm, D), tok),
                  pl.BlockSpec((1, 1, D), per_b),
                  pl.BlockSpec((1, 1, D), per_b),
                  pl.BlockSpec((1, tm // 4, LANES), tok),
                  pl.BlockSpec((1, LANES), lambda b, t: (0, 0)),
                  pl.BlockSpec((D, d_in), lambda b, t: (0, 0), pipeline_mode=pl.Buffered(1)),
                  pl.BlockSpec(rec_lower_bound.shape, lambda b, t: (0, 0, 0))],
        out_specs=[pl.BlockSpec((1, tm, w), tok) for w, _ in outs],
        out_shape=[jax.ShapeDtypeStruct((B, S, w), dt) for w, dt in outs],
        compiler_params=pltpu.CompilerParams(vmem_limit_bytes=VMEM_LIMIT),
        name="proj",
    )(x, sc1, sh1, jnp.repeat(positions.reshape(B, S // 4, 4), LANES // 4, axis=-1), invf, w_in_bf,
      rec_lower_bound)


def _rec_constants():
    C = REC_CHUNK
    nblk = 2 + len(REC_LEVELS)
    mf = np.zeros((nblk, C, C), np.float32)
    r = np.arange(C)
    for t in range(C):
        mf[0, t] = r <= t
        mf[1, t] = r > t
        for i, m in enumerate(REC_LEVELS):
            boundary = (t // (2 * m)) * 2 * m + m
            if t >= boundary:
                mf[2 + i, t] = (r >= boundary) & (r <= t)
            else:
                mf[2 + i, t] = (r > t) & (r <= boundary - 1)
    mb = mf[:, ::-1, ::-1]
    x = r[:, None] ^ r[None, :]
    lev = np.where(x > 0, np.floor(np.log2(np.maximum(x, 1))).astype(np.int32), -1)
    invalid = -2
    levf = np.where(r[:, None] >= r[None, :], lev, invalid).astype(np.int32)
    levb = np.where(r[:, None] <= r[None, :], lev, invalid).astype(np.int32)
    m3 = lambda m: np.concatenate([m.reshape(nblk * C, C)] * 3, axis=1)
    return (jnp.asarray(np.stack([m3(mf), m3(mb)]), BF16),
            jnp.asarray(np.stack([levf, levb])))


REC_HEADS_PER_STEP = 2
REC_UNROLL = 2
REC_STATE_UNROLL = 16


def _rec_kernel(q_ref, kf_ref, kb_ref, lf_ref, lb_ref, v_ref, g_ref, ng_ref, m_ref, lev_ref,
                o_ref, acc_ref, qt_ref, upd_ref, dec_ref):
    C = REC_CHUNK
    HP = REC_HEADS_PER_STEP
    S = q_ref.shape[1]
    n_chunks = S // C
    row = lax.broadcasted_iota(jnp.int32, (C, LANES), 0)
    nt = (((1,), (1,)), ((), ()))
    tn = (((0,), (0,)), ((), ()))
    k_refs = (kf_ref, kb_ref)
    l_refs = (lf_ref, lb_ref)

    def local(i, carry):
        U = REC_UNROLL
        units = [(c, d, h) for c in range(U) for d in range(2) for h in range(HP)]
        ns = {(c, d): (U * i + c if d == 0 else n_chunks - 1 - (U * i + c))
              for c in range(U) for d in range(2)}
        rows = {cd: pl.ds(pl.multiple_of(n * C, C), C) for cd, n in ns.items()}
        sls = [slice(h * LANES, (h + 1) * LANES) for h in range(HP)]
        xs = {}
        for cd in ns:
            lg = l_refs[cd[1]][0, rows[cd], :]
            hi = lg.astype(BF16)
            r1 = lg - hi.astype(F32)
            mid = r1.astype(BF16)
            lo = (r1 - mid.astype(F32)).astype(BF16)
            e = jnp.dot(m_ref[cd[1]], jnp.concatenate([hi, mid, lo], axis=0),
                        preferred_element_type=F32)
            xs[cd] = jnp.exp(e)
        ws, kts, qs, ks, vs = {}, {}, {}, {}, {}
        for c, d in ns:
            cd = (c, d)
            last = C - 1 if d == 0 else 0
            dec_ref[d, pl.ds(ns[cd], 1), :] = xs[cd][last:last + 1]
            for h in range(HP):
                un = (c, d, h)
                q = q_ref[0, rows[cd], sls[h]]
                k = k_refs[d][0, rows[cd], sls[h]]
                qs[un], ks[un], vs[un] = q, k, v_ref[0, rows[cd], sls[h]]
                qf = q.astype(F32)
                kf = k.astype(F32)
                qt_ref[d, rows[cd], sls[h]] = (qf * xs[cd][0:C, sls[h]]).astype(BF16)
                kts[un] = (kf * xs[cd][C:2 * C, sls[h]]).astype(BF16)
                for j, m in enumerate(REC_LEVELS):
                    q_side = ((row & m) != 0) if d == 0 else ((row & m) == 0)
                    ws[un, j] = (jnp.where(q_side, qf, kf)
                                 * xs[cd][(2 + j) * C:(3 + j) * C, sls[h]]).astype(BF16)
        als = {}
        for un in units:
            als[un, -1] = lax.dot_general(qs[un], ks[un], nt, preferred_element_type=F32)
            for j in range(len(REC_LEVELS)):
                als[un, j] = lax.dot_general(ws[un, j], ws[un, j], nt, preferred_element_type=F32)
        for un in units:
            c, d, h = un
            upd_ref[d, h, ns[c, d]] = lax.dot_general(vs[un], kts[un], tn,
                                                      preferred_element_type=F32)
        avs = {}
        for un in units:
            lev = lev_ref[un[1]]
            a = jnp.where(lev == -1, als[un, -1], 0.0)
            for j, m in enumerate(REC_LEVELS):
                a = jnp.where(lev == (m.bit_length() - 1), als[un, j], a)
            avs[un] = a.astype(BF16)
        for un in units:
            c, d, h = un
            acc_ref[d, rows[c, d], sls[h]] = jnp.dot(avs[un], vs[un], preferred_element_type=F32)
        return carry

    lax.fori_loop(0, n_chunks // REC_UNROLL, local, 0)

    for direction in range(2):
        def carried(i, sts, direction=direction):
            n = i if direction == 0 else n_chunks - 1 - i
            rows = pl.ds(pl.multiple_of(n * C, C), C)
            d = dec_ref[direction, pl.ds(n, 1), :]
            out = []
            for h in range(HP):
                sl = slice(h * LANES, (h + 1) * LANES)
                st = sts[h]
                acc_ref[direction, rows, sl] += lax.dot_general(
                    qt_ref[direction, rows, sl], st.astype(BF16), nt, preferred_element_type=F32)
                out.append(st * d[:, sl] + upd_ref[direction, h, n])
            return tuple(out)

        lax.fori_loop(0, n_chunks, carried,
                      tuple(jnp.zeros((REC_DV, REC_DK), F32) for _ in range(HP)),
                      unroll=REC_STATE_UNROLL)

    RB = 256

    def finish(i, carry):
        rows = pl.ds(pl.multiple_of(i * RB, RB), RB)
        for h in range(HP):
            sl = slice(h * LANES, (h + 1) * LANES)
            o = acc_ref[0, rows, sl] + acc_ref[1, rows, sl]
            o = o * lax.rsqrt(jnp.mean(o * o, axis=-1, keepdims=True) + RMS_EPS) * ng_ref[...]
            o_ref[0, rows, sl] = (o * g_ref[0, rows, sl].astype(F32)).astype(BF16)
        return carry

    lax.fori_loop(0, S // RB, finish, 0, unroll=2)


def _rec(q, kf, kb, lf, lb, v, g, norm_g):
    B, S, W = q.shape
    HP = REC_HEADS_PER_STEP
    m3, lev = _rec_constants()
    n_chunks = S // REC_CHUNK
    head = lambda b, h: (b, 0, h)
    blk = pl.BlockSpec((1, S, HP * LANES), head)
    return pl.pallas_call(
        _rec_kernel,
        grid=(B, REC_HEADS // HP),
        in_specs=[blk, blk, blk, blk, blk, blk, blk,
                  pl.BlockSpec((1, REC_DV), lambda b, h: (0, 0)),
                  pl.BlockSpec(m3.shape, lambda b, h: (0, 0, 0)),
                  pl.BlockSpec(lev.shape, lambda b, h: (0, 0, 0))],
        out_specs=blk,
        out_shape=jax.ShapeDtypeStruct((B, S, W), BF16),
        scratch_shapes=[pltpu.VMEM((2, S, HP * REC_DV), F32),
                        pltpu.VMEM((2, S, HP * REC_DK), BF16),
                        pltpu.VMEM((2, HP, n_chunks, REC_DV, REC_DK), F32),
                        pltpu.VMEM((2, n_chunks, HP * REC_DK), F32)],
        compiler_params=pltpu.CompilerParams(vmem_limit_bytes=VMEM_LIMIT),
        name="rec",
    )(q, kf, kb, lf, lb, v, g, norm_g.reshape(1, REC_DV), m3, lev)


def _att_bias(S):
    rowi = np.arange(ATT_BLOCK)[:, None]
    coli = np.arange(3 * ATT_BLOCK)[None, :]
    out = []
    for off in (0, ATT_BLOCK, 2 * ATT_BLOCK):
        ok = np.abs(coli - off - rowi) <= ATT_WINDOW
        m = np.where(ok, 0.0, NEG_BIG).reshape(ATT_BLOCK, 3, 1, ATT_BLOCK)
        out.append(np.broadcast_to(m, (ATT_BLOCK, 3, 2, ATT_BLOCK)).reshape(ATT_BLOCK, 6 * ATT_BLOCK))
    return jnp.asarray(np.stack(out), F32)


def _mix_kernel(x_ref, ga_ref, sc_ref, sh_ref, aq_ref, ak_ref, av_ref, orec_ref, gr_ref, gt_ref,
                sink_ref, bias_ref, wrec_ref, watt_ref, wout_ref, lg_ref, lb_ref,
                x1_o, u2_o, kvar, vvar, att_scr, yrec_scr):
    t = pl.program_id(1)
    S = ak_ref.shape[1]
    TQ = x_ref.shape[1]
    nb = S // ATT_BLOCK
    KW = 3 * ATT_BLOCK
    HALF = LANES // 2
    nt = (((1,), (1,)), ((), ()))

    @pl.when(t == 0)
    def _():
        lane = lax.broadcasted_iota(jnp.int32, (S, LANES), 1)
        lo = lane < HALF
        k = ak_ref[0].astype(F32)
        v = av_ref[0].astype(F32)
        ksw = pltpu.roll(k, HALF, axis=1)
        vsw = pltpu.roll(v, HALF, axis=1)
        zero = jnp.zeros_like(k)
        for i, (kk, vv) in enumerate([(jnp.where(lo, k, zero), jnp.where(lo, v, zero)),
                                      (jnp.where(lo, zero, ksw), jnp.where(lo, zero, vsw)),
                                      (jnp.where(lo, ksw, zero), jnp.where(lo, vsw, zero)),
                                      (jnp.where(lo, zero, k), jnp.where(lo, zero, v))]):
            ones = jnp.where(lo, 1.0, 0.0) if i % 2 == 0 else jnp.where(lo, 0.0, 1.0)
            vo = jnp.concatenate([vv, ones], axis=1).astype(BF16)
            kvar[i // 2, :, i % 2] = kk.astype(BF16).reshape(nb, ATT_BLOCK, LANES)
            vvar[i // 2, :, i % 2] = vo.reshape(nb, ATT_BLOCK, 2 * LANES)

    lane_q = lax.broadcasted_iota(jnp.int32, (ATT_BLOCK, LANES), 1)
    lo_q = lane_q < HALF
    n_blocks = TQ // ATT_BLOCK

    def attend_scores(jj):
        rows = slice(jj * ATT_BLOCK, (jj + 1) * ATT_BLOCK)
        j = t * n_blocks + jj
        kb = jnp.clip(j - 1, 0, nb - 3)
        bidx = jnp.where(j == 0, 0, jnp.where(j == nb - 1, 2, 1))
        bias = bias_ref[bidx]
        scs = {}
        for s in range(ATT_QWIDTH // LANES):
            hk = (2 * s) // (ATT_Q_HEADS // ATT_KV_HEADS)
            qs = aq_ref[0, rows, s * LANES:(s + 1) * LANES]
            kwin = kvar[hk, pl.ds(kb, 3)].reshape(2 * KW, LANES)
            scs[s] = lax.dot_general(qs, kwin, nt, preferred_element_type=F32) + bias
        return kb, scs

    def attend_finish(jj, kb, scs):
        rows = slice(jj * ATT_BLOCK, (jj + 1) * ATT_BLOCK)
        for s in range(ATT_QWIDTH // LANES):
            hk = (2 * s) // (ATT_Q_HEADS // ATT_KV_HEADS)
            sc = scs[s]
            blocks = [sc[:, i * ATT_BLOCK:(i + 1) * ATT_BLOCK] for i in range(6)]
            ps, den_sink = list(blocks), []
            for par in range(2):
                sk = sink_ref[0, 2 * s + par] * LOG2E
                m3 = jnp.maximum(jnp.maximum(blocks[par], blocks[2 + par]), blocks[4 + par])
                mx = jnp.maximum(jnp.max(m3, axis=-1, keepdims=True), sk)
                for i in range(par, 6, 2):
                    ps[i] = jnp.exp2(blocks[i] - mx).astype(BF16)
                den_sink.append(jnp.exp2(sk - mx))
            vwin = vvar[hk, pl.ds(kb, 3)].reshape(2 * KW, 2 * LANES)
            acc = jnp.dot(jnp.concatenate(ps, axis=1), vwin, preferred_element_type=F32)
            den = acc[:, LANES:] + jnp.where(lo_q, den_sink[0], den_sink[1])
            att_scr[rows, s * LANES:(s + 1) * LANES] = (acc[:, :LANES] / den).astype(BF16)

    def dense(jj):
        rows = slice(jj * ATT_BLOCK, (jj + 1) * ATT_BLOCK)
        y_att = jnp.dot(att_scr[rows, :], watt_ref[...], preferred_element_type=F32)
        merged = (gr_ref[0, rows, :].astype(F32) * yrec_scr[rows, :]
                  + gt_ref[0, rows, :].astype(F32) * y_att)
        z = jnp.dot(merged.astype(BF16), wout_ref[...], preferred_element_type=F32)
        r = DEEPNORM_ALPHA * x_ref[0, rows, :] + (1.0 + ga_ref[0]) * z
        x1 = _layer_norm(r, lg_ref[...], lb_ref[...])
        x1_o[0, rows, :] = x1
        u2_o[0, rows, :] = (x1 * (1.0 + sc_ref[0]) + sh_ref[0]).astype(BF16)

    yrec_scr[...] = jnp.dot(orec_ref[0], wrec_ref[...], preferred_element_type=F32)
    attend_finish(0, *attend_scores(0))
    for jj in range(n_blocks):
        if jj + 1 < n_blocks:
            kb, scs = attend_scores(jj + 1)
        dense(jj)
        if jj + 1 < n_blocks:
            attend_finish(jj + 1, kb, scs)


def _mix(x, ga1, sc2, sh2, aq, ak, av, orec, g_rec, g_att, sink, w_rec, w_att, w_out, ln_g, ln_b, tq):
    B, S, D = x.shape
    bias = _att_bias(S)
    tok = lambda b, t: (b, t, 0)
    per_b = lambda b, t: (b, 0, 0)
    c2 = lambda b, t: (0, 0)
    full = lambda a: pl.BlockSpec(a.shape, (lambda b, t: (0,) * a.ndim))
    return pl.pallas_call(
        _mix_kernel,
        grid=(B, S // tq),
        in_specs=[pl.BlockSpec((1, tq, D), tok),
                  pl.BlockSpec((1, 1, D), per_b), pl.BlockSpec((1, 1, D), per_b),
                  pl.BlockSpec((1, 1, D), per_b),
                  pl.BlockSpec((1, tq, ATT_QWIDTH), tok),
                  pl.BlockSpec((1, S, ATT_KVWIDTH), per_b),
                  pl.BlockSpec((1, S, ATT_KVWIDTH), per_b),
                  pl.BlockSpec((1, tq, REC_WIDTH), tok),
                  pl.BlockSpec((1, tq, D), tok), pl.BlockSpec((1, tq, D), tok),
                  pl.BlockSpec(memory_space=pltpu.SMEM),
                  full(bias), full(w_rec), full(w_att), full(w_out),
                  pl.BlockSpec((1, D), c2), pl.BlockSpec((1, D), c2)],
        out_specs=[pl.BlockSpec((1, tq, D), tok), pl.BlockSpec((1, tq, D), tok)],
        out_shape=[jax.ShapeDtypeStruct((B, S, D), F32), jax.ShapeDtypeStruct((B, S, D), BF16)],
        scratch_shapes=[pltpu.VMEM((ATT_KV_HEADS, S // ATT_BLOCK, 2, ATT_BLOCK, LANES), BF16),
                        pltpu.VMEM((ATT_KV_HEADS, S // ATT_BLOCK, 2, ATT_BLOCK, 2 * LANES), BF16),
                        pltpu.VMEM((tq, ATT_QWIDTH), BF16),
                        pltpu.VMEM((tq, D), F32)],
        compiler_params=pltpu.CompilerParams(
            dimension_semantics=("arbitrary", "arbitrary"), vmem_limit_bytes=VMEM_LIMIT),
        name="mix",
    )(x, ga1, sc2, sh2, aq, ak, av, orec, g_rec, g_att, sink.reshape(1, ATT_Q_HEADS), bias,
      w_rec, w_att, w_out, ln_g.reshape(1, D), ln_b.reshape(1, D))


HALO = 16
FFN_CHUNK = 256
FFN_ROWS = 128


def _gelu_tanh(x):
    return 0.5 * x * (1.0 + jnp.tanh(np.sqrt(2.0 / np.pi).astype(np.float32) * (x + 0.044715 * (x * x * x))))


def _ffn_kernel(u_ref, up_ref, un_ref, x1_ref, ga_ref, wup_ref, cw_ref, cb_ref, wdn_ref,
                lg_ref, lb_ref, o_ref, ucat, h_scr, acc_ref):
    t = pl.program_id(1)
    nt_ = pl.num_programs(1)
    tm = u_ref.shape[1]
    d_ff = wdn_ref.shape[0]
    ucat[0:HALO] = jnp.where(t > 0, up_ref[0], jnp.zeros_like(up_ref[0]))
    ucat[HALO:HALO + tm] = u_ref[0]
    ucat[HALO + tm:] = jnp.where(t < nt_ - 1, un_ref[0], jnp.zeros_like(un_ref[0]))
    u = ucat[...]

    n_chunks = d_ff // FFN_CHUNK

    def up(j, part):
        if j < n_chunks:
            col = part * d_ff + j * FFN_CHUNK
            h = jnp.dot(u, wup_ref[:, col:col + FFN_CHUNK], preferred_element_type=F32)
            for s in range(FFN_CHUNK // LANES):
                h_scr[j % 2, part, s] = h[:, s * LANES:(s + 1) * LANES]

    def conv(j, part, r0):
        slabs = []
        for s in range(FFN_CHUNK // LANES):
            col = part * d_ff + j * FFN_CHUNK + s * LANES
            w = cw_ref[:, col:col + LANES]
            out = cb_ref[:, col:col + LANES]
            for tap in range(CONV_K):
                lo = HALO + tap - CONV_K // 2 + r0
                out = out + h_scr[j % 2, part, s, lo:lo + FFN_ROWS, :] * w[tap:tap + 1]
            slabs.append(out)
        return jnp.concatenate(slabs, axis=1)

    def down(j, r0):
        rows = slice(r0, r0 + FFN_ROWS)
        a = (_gelu_tanh(conv(j, 1, r0)) * conv(j, 0, r0)).astype(BF16)
        part = jnp.dot(a, wdn_ref[j * FFN_CHUNK:(j + 1) * FFN_CHUNK, :], preferred_element_type=F32)
        if j == 0:
            acc_ref[rows, :] = part
        elif j < n_chunks - 1:
            acc_ref[rows, :] += part
        else:
            r = DEEPNORM_ALPHA * x1_ref[0, rows, :] + (1.0 + ga_ref[0]) * (acc_ref[rows, :] + part)
            o_ref[0, rows, :] = _layer_norm(r, lg_ref[...], lb_ref[...])

    up(0, 0)
    up(0, 1)
    row_blocks = list(range(0, tm, FFN_ROWS))
    half = len(row_blocks) // 2
    for j in range(n_chunks):
        up(j + 1, 0)
        for r0 in row_blocks[:half]:
            down(j, r0)
        up(j + 1, 1)
        for r0 in row_blocks[half:]:
            down(j, r0)


def _ffn(u2, x1, ga2, w_up, conv_w, conv_b, w_down, ln_g, ln_b, tm):
    B, S, D = x1.shape
    d_ff = w_down.shape[0]
    assert d_ff % FFN_CHUNK == 0 and tm % HALO == 0
    nh = tm // HALO
    last = S // HALO - 1
    tok = lambda b, t: (b, t, 0)
    per_b = lambda b, t: (b, 0, 0)
    c2 = lambda b, t: (0, 0)
    return pl.pallas_call(
        _ffn_kernel,
        grid=(B, S // tm),
        in_specs=[pl.BlockSpec((1, tm, D), tok),
                  pl.BlockSpec((1, HALO, D), lambda b, t: (b, jnp.maximum(t * nh - 1, 0), 0)),
                  pl.BlockSpec((1, HALO, D), lambda b, t: (b, jnp.minimum((t + 1) * nh, last), 0)),
                  pl.BlockSpec((1, tm, D), tok),
                  pl.BlockSpec((1, 1, D), per_b),
                  pl.BlockSpec(w_up.shape, c2, pipeline_mode=pl.Buffered(1)),
                  pl.BlockSpec(conv_w.shape, c2),
                  pl.BlockSpec((1, 2 * d_ff), c2),
                  pl.BlockSpec(w_down.shape, c2, pipeline_mode=pl.Buffered(1)),
                  pl.BlockSpec((1, D), c2), pl.BlockSpec((1, D), c2)],
        out_specs=pl.BlockSpec((1, tm, D), tok),
        out_shape=jax.ShapeDtypeStruct((B, S, D), F32),
        scratch_shapes=[pltpu.VMEM((tm + 2 * HALO, D), BF16),
                        pltpu.VMEM((2, 2, FFN_CHUNK // LANES, tm + 2 * HALO, LANES), F32),
                        pltpu.VMEM((tm, D), F32)],
        compiler_params=pltpu.CompilerParams(vmem_limit_bytes=VMEM_LIMIT),
        name="ffn",
    )(u2, u2, u2, x1, ga2, w_up, conv_w, conv_b.reshape(1, 2 * d_ff), w_down,
      ln_g.reshape(1, D), ln_b.reshape(1, D))


def kernel(x, c, positions, w_ada, b_ada, w_in, rec_lower_bound, rec_norm_g, w_rec_branch,
           attn_sink, w_attn_branch, w_out, ln1_g, ln1_b, w_up, conv_w, conv_b, w_down,
           ln2_g, ln2_b):
    B, S, D = x.shape
    assert w_ada.shape[0] == DEPTH and rec_lower_bound.shape[1] == DEPTH + 1
    tm_proj = min(1024, S)
    tq = min(1024, S)
    tm_ffn = min(1024, S)
    for l in range(DEPTH):
        mods = _ada(c, w_ada[l], b_ada[l])
        sh1, sc1, ga1, sh2, sc2, ga2 = [m.reshape(B, 1, D) for m in jnp.split(mods, 6, axis=-1)]
        (q, kf, kb, lf, lb, v, g, aq, ak, av, g_rec, g_att) = _proj(
            x, sc1, sh1, positions, w_in[l].astype(BF16), rec_lower_bound, tm_proj)
        orec = _rec(q, kf, kb, lf, lb, v, g, rec_norm_g[l])
        x1, u2 = _mix(x, ga1, sc2, sh2, aq, ak, av, orec, g_rec, g_att, attn_sink[l],
                      w_rec_branch[l].astype(BF16), w_attn_branch[l].astype(BF16),
                      w_out[l].astype(BF16), ln1_g[l], ln1_b[l], tq)
        x = _ffn(u2, x1, ga2, w_up[l].astype(BF16), conv_w[l], conv_b[l], w_down[l].astype(BF16),
                 ln2_g[l], ln2_b[l], tm_ffn)
    return x
```

```python
import numpy as np
import jax
import jax.numpy as jnp
from jax import lax
from jax.experimental import pallas as pl
from jax.experimental.pallas import tpu as pltpu

F32 = jnp.float32
BF16 = jnp.bfloat16

REC_HEADS = 4
REC_DK = 128
REC_DV = 128
REC_WIDTH = REC_HEADS * REC_DK
ATT_Q_HEADS = 8
ATT_KV_HEADS = 2
ATT_HEAD_DIM = 64
ATT_QWIDTH = ATT_Q_HEADS * ATT_HEAD_DIM
ATT_KVWIDTH = ATT_KV_HEADS * ATT_HEAD_DIM
ATT_WINDOW = 128
ATT_BLOCK = 128
ROPE_THETA = 10000.0
CONV_K = 3
LN_EPS = 1e-5
RMS_EPS = 1e-6
DEPTH = 1
DEEPNORM_ALPHA = (2 * DEPTH) ** 0.25

LANES = 128
REC_CHUNK = 64
REC_LEVELS = (32, 16, 8, 4, 2, 1)
REC_MXU_LEVELS = (4, 2, 1)
LOG2E = 1.4426950408889634
NEG_BIG = -1e30
VMEM_LIMIT = 56 * 1024 * 1024


def _sigmoid(x):
    return 1.0 / (1.0 + jnp.exp(-x))


def _silu(x):
    return x * _sigmoid(x)


def _layer_norm(r, g, b):
    mu = jnp.mean(r, axis=-1, keepdims=True)
    rc = r - mu
    var = jnp.mean(rc * rc, axis=-1, keepdims=True)
    return rc * lax.rsqrt(var + LN_EPS) * g + b


def _ada_kernel(c_ref, w_ref, b_ref, o_ref):
    c = c_ref[...]
    o_ref[...] = jnp.dot(_silu(c), w_ref[...], preferred_element_type=F32) + b_ref[...]


def _ada(c, w_ada, b_ada):
    B, D = c.shape
    N = w_ada.shape[1]
    return pl.pallas_call(
        _ada_kernel,
        grid=(N // D,),
        in_specs=[pl.BlockSpec((B, D), lambda j: (0, 0)),
                  pl.BlockSpec((D, D), lambda j: (0, j)),
                  pl.BlockSpec((1, D), lambda j: (0, j))],
        out_specs=pl.BlockSpec((B, D), lambda j: (0, j)),
        out_shape=jax.ShapeDtypeStruct((B, N), F32),
        name="ada",
    )(c, w_ada, b_ada.reshape(1, N))


PROJ_COLS = 256


def _proj_kernel(x_ref, sc_ref, sh_ref, pos_ref, invf_ref, w_ref, rlb_ref,
                 q_o, kf_o, kb_o, lf_o, lb_o, v_o, g_o, aq_o, ak_o, av_o, gr_o, ga_o):
    u = (x_ref[0] * (1.0 + sc_ref[0]) + sh_ref[0]).astype(BF16)
    D = x_ref.shape[-1]
    W = REC_WIDTH

    def pieces(base, width):
        for c in range(0, width, PROJ_COLS):
            yield slice(c, c + PROJ_COLS), jnp.dot(
                u, w_ref[:, base + c:base + c + PROJ_COLS], preferred_element_type=F32)

    for sl, y in pieces(0, W):
        q_o[0, :, sl] = (_silu(y) * (REC_DK ** -0.5)).astype(BF16)

    for direction, (k_o, l_o) in enumerate(((kf_o, lf_o), (kb_o, lb_o))):
        r = rlb_ref[direction]
        e = jnp.exp(r - jnp.max(r, axis=0, keepdims=True))
        lbd = e[0:1] / jnp.sum(e, axis=0, keepdims=True)
        for sl, y in pieces((1 + direction) * W, W):
            f = lbd[:, sl] + (1.0 - lbd[:, sl]) * _sigmoid(y)
            k_o[0, :, sl] = (1.0 - f).astype(BF16)
            l_o[0, :, sl] = jnp.log(f)

    for sl, y in pieces(3 * W, W):
        v_o[0, :, sl] = y.astype(BF16)
    for sl, y in pieces(4 * W, W):
        g_o[0, :, sl] = _silu(y).astype(BF16)

    quarter = LANES // 4
    ang_c = pos_ref[0].astype(F32) * invf_ref[...]
    lane = lax.broadcasted_iota(jnp.int32, (x_ref.shape[1], LANES), 1)
    rowm = lax.broadcasted_iota(jnp.int32, (x_ref.shape[1], LANES), 0) % 4

    def expand(tc):
        rep = jnp.repeat(tc, 4, axis=0)
        y = rep
        for j in range(1, 4):
            y = jnp.where(rowm == j, pltpu.roll(rep, LANES - j * quarter, axis=1), y)
        z = jnp.where(lane < quarter, y, 0.0)
        z = z + pltpu.roll(z, quarter, axis=1)
        return z + pltpu.roll(z, 2 * quarter, axis=1)

    cos = expand(jnp.cos(ang_c))
    sin = expand(jnp.sin(ang_c))
    first_half = (lane % ATT_HEAD_DIM) < (ATT_HEAD_DIM // 2)
    sin_signed = jnp.where(first_half, -sin, sin)

    def rope(t):
        partner = jnp.where(first_half,
                            pltpu.roll(t, LANES - ATT_HEAD_DIM // 2, axis=1),
                            pltpu.roll(t, ATT_HEAD_DIM // 2, axis=1))
        return t * cos + partner * sin_signed

    c0 = 5 * W
    for sl, y in pieces(c0, ATT_QWIDTH):
        for s in range(PROJ_COLS // LANES):
            lanes = slice(sl.start + s * LANES, sl.start + (s + 1) * LANES)
            aq_o[0, :, lanes] = (rope(y[:, s * LANES:(s + 1) * LANES])
                                 * (ATT_HEAD_DIM ** -0.5 * LOG2E)).astype(BF16)
    c0 += ATT_QWIDTH
    assert 2 * ATT_KVWIDTH == PROJ_COLS
    for _, y in pieces(c0, 2 * ATT_KVWIDTH):
        ak_o[0] = rope(y[:, :ATT_KVWIDTH]).astype(BF16)
        av_o[0] = y[:, ATT_KVWIDTH:].astype(BF16)
    c0 += 2 * ATT_KVWIDTH
    for sl, y in pieces(c0, D):
        gr_o[0, :, sl] = _sigmoid(y).astype(BF16)
    for sl, y in pieces(c0 + D, D):
        ga_o[0, :, sl] = _sigmoid(y).astype(BF16)


def _proj(x, sc1, sh1, positions, w_in_bf, rec_lower_bound, tm):
    B, S, D = x.shape
    d_in = w_in_bf.shape[1]
    half = ATT_HEAD_DIM // 2
    inv_freq = ROPE_THETA ** (-jnp.arange(half, dtype=F32) / half)
    invf = jnp.tile(inv_freq, LANES // half).reshape(1, LANES)
    W = REC_WIDTH
    tok = lambda b, t: (b, t, 0)
    per_b = lambda b, t: (b, 0, 0)
    outs = [(W, BF16), (W, BF16), (W, BF16), (W, F32), (W, F32), (W, BF16), (W, BF16),
            (ATT_QWIDTH, BF16), (ATT_KVWIDTH, BF16), (ATT_KVWIDTH, BF16), (D, BF16), (D, BF16)]
    return pl.pallas_call(
        _proj_kernel,
        grid=(B, S // tm),
        in_specs=[pl.BlockSpec((1, tm, D), tok),
                  pl.BlockSpec((1, 1, D), per_b),
                  pl.BlockSpec((1, 1, D), per_b),
                  pl.BlockSpec((1, tm // 4, LANES), tok),
                  pl.BlockSpec((1, LANES), lambda b, t: (0, 0)),
                  pl.BlockSpec((D, d_in), lambda b, t: (0, 0), pipeline_mode=pl.Buffered(1)),
                  pl.BlockSpec(rec_lower_bound.shape, lambda b, t: (0, 0, 0))],
        out_specs=[pl.BlockSpec((1, tm, w), tok) for w, _ in outs],
        out_shape=[jax.ShapeDtypeStruct((B, S, w), dt) for w, dt in outs],
        compiler_params=pltpu.CompilerParams(vmem_limit_bytes=VMEM_LIMIT),
        name="proj",
    )(x, sc1, sh1, jnp.repeat(positions.reshape(B, S // 4, 4), LANES // 4, axis=-1), invf, w_in_bf,
      rec_lower_bound)


def _rec_constants():
    C = REC_CHUNK
    nblk = 1 + len(REC_MXU_LEVELS)
    mf = np.zeros((nblk, C, C), np.float32)
    r = np.arange(C)
    for t in range(C):
        mf[0, t] = r <= t
        for i, m in enumerate(REC_MXU_LEVELS):
            boundary = (t // (2 * m)) * 2 * m + m
            if t >= boundary:
                mf[1 + i, t] = (r >= boundary) & (r <= t)
            else:
                mf[1 + i, t] = (r > t) & (r <= boundary - 1)
    mb = mf[:, ::-1, ::-1]
    x = r[:, None] ^ r[None, :]
    lev = np.where(x > 0, np.floor(np.log2(np.maximum(x, 1))).astype(np.int32), -1)
    invalid = -2
    levf = np.where(r[:, None] >= r[None, :], lev, invalid).astype(np.int32)
    levb = np.where(r[:, None] <= r[None, :], lev, invalid).astype(np.int32)
    m3 = lambda m: np.concatenate([m.reshape(nblk * C, C)] * 3, axis=1)
    return (jnp.asarray(np.stack([m3(mf), m3(mb)]), BF16),
            jnp.asarray(np.stack([levf, levb])))


REC_HEADS_PER_STEP = 2
REC_UNROLL = 2
REC_STATE_UNROLL = 32


def _rec_kernel(q_ref, kf_ref, kb_ref, lf_ref, lb_ref, v_ref, g_ref, ng_ref, m_ref, lev_ref,
                o_ref, acc_ref, qt_ref, upd_ref, dec_ref):
    C = REC_CHUNK
    HP = REC_HEADS_PER_STEP
    S = q_ref.shape[1]
    n_chunks = S // C
    row = lax.broadcasted_iota(jnp.int32, (C, LANES), 0)
    nt = (((1,), (1,)), ((), ()))
    tn = (((0,), (0,)), ((), ()))
    k_refs = (kf_ref, kb_ref)
    l_refs = (lf_ref, lb_ref)

    def local(i, carry):
        U = REC_UNROLL
        units = [(c, d, h) for c in range(U) for d in range(2) for h in range(HP)]
        ns = {(c, d): (U * i + c if d == 0 else n_chunks - 1 - (U * i + c))
              for c in range(U) for d in range(2)}
        rows = {cd: pl.ds(pl.multiple_of(n * C, C), C) for cd, n in ns.items()}
        sls = [slice(h * LANES, (h + 1) * LANES) for h in range(HP)]
        xs = {}
        for cd in ns:
            lg = l_refs[cd[1]][0, rows[cd], :]
            hi = lg.astype(BF16)
            r1 = lg - hi.astype(F32)
            mid = r1.astype(BF16)
            lo = (r1 - mid.astype(F32)).astype(BF16)
            e = jnp.dot(m_ref[cd[1]], jnp.concatenate([hi, mid, lo], axis=0),
                        preferred_element_type=F32)
            d = cd[1]
            b = e[0:C]
            last = C - 1 if d == 0 else 0
            ex = [b, b[last:last + 1] - b]
            for m in REC_LEVELS:
                if m in REC_MXU_LEVELS:
                    j = REC_MXU_LEVELS.index(m)
                    ex.append(e[(1 + j) * C:(2 + j) * C])
                    continue
                parts = []
                for r0 in range(0, C, 8):
                    base = (r0 // (2 * m)) * 2 * m + m
                    ref_row = base - 1 if d == 0 else base
                    q_side = ((r0 & m) != 0) if d == 0 else ((r0 & m) == 0)
                    ref = b[ref_row:ref_row + 1]
                    parts.append(b[r0:r0 + 8] - ref if q_side else ref - b[r0:r0 + 8])
                ex.append(jnp.concatenate(parts, axis=0))
            xs[cd] = jnp.exp(jnp.concatenate(ex, axis=0))
        ws, kts, qs, ks, vs = {}, {}, {}, {}, {}
        for c, d in ns:
            cd = (c, d)
            last = C - 1 if d == 0 else 0
            dec_ref[d, pl.ds(ns[cd], 1), :] = xs[cd][last:last + 1]
            for h in range(HP):
                un = (c, d, h)
                q = q_ref[0, rows[cd], sls[h]]
                k = k_refs[d][0, rows[cd], sls[h]]
                qs[un], ks[un], vs[un] = q, k, v_ref[0, rows[cd], sls[h]]
                qf = q.astype(F32)
                kf = k.astype(F32)
                qt_ref[d, rows[cd], sls[h]] = (qf * xs[cd][0:C, sls[h]]).astype(BF16)
                kts[un] = (kf * xs[cd][C:2 * C, sls[h]]).astype(BF16)
                for j, m in enumerate(REC_LEVELS):
                    q_side = ((row & m) != 0) if d == 0 else ((row & m) == 0)
                    ws[un, j] = (jnp.where(q_side, qf, kf)
                                 * xs[cd][(2 + j) * C:(3 + j) * C, sls[h]]).astype(BF16)
        als = {}
        for un in units:
            als[un, -1] = lax.dot_general(qs[un], ks[un], nt, preferred_element_type=F32)
            for j in range(len(REC_LEVELS)):
                als[un, j] = lax.dot_general(ws[un, j], ws[un, j], nt, preferred_element_type=F32)
        for un in units:
            c, d, h = un
            upd_ref[d, h, ns[c, d]] = lax.dot_general(vs[un], kts[un], tn,
                                                      preferred_element_type=F32)
        avs = {}
        for un in units:
            lev = lev_ref[un[1]]
            a = jnp.where(lev == -1, als[un, -1], 0.0)
            for j, m in enumerate(REC_LEVELS):
                a = jnp.where(lev == (m.bit_length() - 1), als[un, j], a)
            avs[un] = a.astype(BF16)
        for un in units:
            c, d, h = un
            acc_ref[d, rows[c, d], sls[h]] = jnp.dot(avs[un], vs[un], preferred_element_type=F32)
        return carry

    lax.fori_loop(0, n_chunks // REC_UNROLL, local, 0)

    for direction in range(2):
        def carried(i, sts, direction=direction):
            n = i if direction == 0 else n_chunks - 1 - i
            rows = pl.ds(pl.multiple_of(n * C, C), C)
            d = dec_ref[direction, pl.ds(n, 1), :]
            out = []
            for h in range(HP):
                sl = slice(h * LANES, (h + 1) * LANES)
                st = sts[h]
                acc_ref[direction, rows, sl] += lax.dot_general(
                    qt_ref[direction, rows, sl], st.astype(BF16), nt, preferred_element_type=F32)
                out.append(st * d[:, sl] + upd_ref[direction, h, n])
            return tuple(out)

        lax.fori_loop(0, n_chunks, carried,
                      tuple(jnp.zeros((REC_DV, REC_DK), F32) for _ in range(HP)),
                      unroll=REC_STATE_UNROLL)

    RB = 256

    def finish(i, carry):
        rows = pl.ds(pl.multiple_of(i * RB, RB), RB)
        for h in range(HP):
            sl = slice(h * LANES, (h + 1) * LANES)
            o = acc_ref[0, rows, sl] + acc_ref[1, rows, sl]
            o = o * lax.rsqrt(jnp.mean(o * o, axis=-1, keepdims=True) + RMS_EPS) * ng_ref[...]
            o_ref[0, rows, sl] = (o * g_ref[0, rows, sl].astype(F32)).astype(BF16)
        return carry

    lax.fori_loop(0, S // RB, finish, 0, unroll=4)


def _rec(q, kf, kb, lf, lb, v, g, norm_g):
    B, S, W = q.shape
    HP = REC_HEADS_PER_STEP
    m3, lev = _rec_constants()
    n_chunks = S // REC_CHUNK
    head = lambda b, h: (b, 0, h)
    blk = pl.BlockSpec((1, S, HP * LANES), head)
    return pl.pallas_call(
        _rec_kernel,
        grid=(B, REC_HEADS // HP),
        in_specs=[blk, blk, blk, blk, blk, blk, blk,
                  pl.BlockSpec((1, REC_DV), lambda b, h: (0, 0)),
                  pl.BlockSpec(m3.shape, lambda b, h: (0, 0, 0)),
                  pl.BlockSpec(lev.shape, lambda b, h: (0, 0, 0))],
        out_specs=blk,
        out_shape=jax.ShapeDtypeStruct((B, S, W), BF16),
        scratch_shapes=[pltpu.VMEM((2, S, HP * REC_DV), F32),
                        pltpu.VMEM((2, S, HP * REC_DK), BF16),
                        pltpu.VMEM((2, HP, n_chunks, REC_DV, REC_DK), F32),
                        pltpu.VMEM((2, n_chunks, HP * REC_DK), F32)],
        compiler_params=pltpu.CompilerParams(vmem_limit_bytes=VMEM_LIMIT),
        name="rec",
    )(q, kf, kb, lf, lb, v, g, norm_g.reshape(1, REC_DV), m3, lev)


def _att_bias(S):
    rowi = np.arange(ATT_BLOCK)[:, None]
    coli = np.arange(3 * ATT_BLOCK)[None, :]
    out = []
    for off in (0, ATT_BLOCK, 2 * ATT_BLOCK):
        ok = np.abs(coli - off - rowi) <= ATT_WINDOW
        m = np.where(ok, 0.0, NEG_BIG).reshape(ATT_BLOCK, 3, 1, ATT_BLOCK)
        out.append(np.broadcast_to(m, (ATT_BLOCK, 3, 2, ATT_BLOCK)).reshape(ATT_BLOCK, 6 * ATT_BLOCK))
    return jnp.asarray(np.stack(out), F32)


def _mix_kernel(x_ref, ga_ref, sc_ref, sh_ref, aq_ref, ak_ref, av_ref, orec_ref, gr_ref, gt_ref,
                sink_ref, bias_ref, wrec_ref, watt_ref, wout_ref, lg_ref, lb_ref,
                x1_o, u2_o, kvar, vvar, att_scr, yrec_scr):
    t = pl.program_id(1)
    S = ak_ref.shape[1]
    TQ = x_ref.shape[1]
    nb = S // ATT_BLOCK
    KW = 3 * ATT_BLOCK
    HALF = LANES // 2
    nt = (((1,), (1,)), ((), ()))

    @pl.when(t == 0)
    def _():
        lane = lax.broadcasted_iota(jnp.int32, (S, LANES), 1)
        lo = lane < HALF
        k = ak_ref[0].astype(F32)
        v = av_ref[0].astype(F32)
        ksw = pltpu.roll(k, HALF, axis=1)
        vsw = pltpu.roll(v, HALF, axis=1)
        zero = jnp.zeros_like(k)
        for i, (kk, vv) in enumerate([(jnp.where(lo, k, zero), jnp.where(lo, v, zero)),
                                      (jnp.where(lo, zero, ksw), jnp.where(lo, zero, vsw)),
                                      (jnp.where(lo, ksw, zero), jnp.where(lo, vsw, zero)),
                                      (jnp.where(lo, zero, k), jnp.where(lo, zero, v))]):
            ones = jnp.where(lo, 1.0, 0.0) if i % 2 == 0 else jnp.where(lo, 0.0, 1.0)
            vo = jnp.concatenate([vv, ones], axis=1).astype(BF16)
            kvar[i // 2, :, i % 2] = kk.astype(BF16).reshape(nb, ATT_BLOCK, LANES)
            vvar[i // 2, :, i % 2] = vo.reshape(nb, ATT_BLOCK, 2 * LANES)

    lane_q = lax.broadcasted_iota(jnp.int32, (ATT_BLOCK, LANES), 1)
    lo_q = lane_q < HALF
    n_blocks = TQ // ATT_BLOCK

    def attend_scores(jj):
        rows = slice(jj * ATT_BLOCK, (jj + 1) * ATT_BLOCK)
        j = t * n_blocks + jj
        kb = jnp.clip(j - 1, 0, nb - 3)
        bidx = jnp.where(j == 0, 0, jnp.where(j == nb - 1, 2, 1))
        bias = bias_ref[bidx]
        scs = {}
        for s in range(ATT_QWIDTH // LANES):
            hk = (2 * s) // (ATT_Q_HEADS // ATT_KV_HEADS)
            qs = aq_ref[0, rows, s * LANES:(s + 1) * LANES]
            kwin = kvar[hk, pl.ds(kb, 3)].reshape(2 * KW, LANES)
            scs[s] = lax.dot_general(qs, kwin, nt, preferred_element_type=F32) + bias
        return kb, scs

    def attend_finish(jj, kb, scs):
        rows = slice(jj * ATT_BLOCK, (jj + 1) * ATT_BLOCK)
        for s in range(ATT_QWIDTH // LANES):
            hk = (2 * s) // (ATT_Q_HEADS // ATT_KV_HEADS)
            sc = scs[s]
            blocks = [sc[:, i * ATT_BLOCK:(i + 1) * ATT_BLOCK] for i in range(6)]
            ps, den_sink = list(blocks), []
            for par in range(2):
                sk = sink_ref[0, 2 * s + par] * LOG2E
                m3 = jnp.maximum(jnp.maximum(blocks[par], blocks[2 + par]), blocks[4 + par])
                mx = jnp.maximum(jnp.max(m3, axis=-1, keepdims=True), sk)
                for i in range(par, 6, 2):
                    ps[i] = jnp.exp2(blocks[i] - mx).astype(BF16)
                den_sink.append(jnp.exp2(sk - mx))
            vwin = vvar[hk, pl.ds(kb, 3)].reshape(2 * KW, 2 * LANES)
            acc = jnp.dot(jnp.concatenate(ps, axis=1), vwin, preferred_element_type=F32)
            den = acc[:, LANES:] + jnp.where(lo_q, den_sink[0], den_sink[1])
            att_scr[rows, s * LANES:(s + 1) * LANES] = (acc[:, :LANES] / den).astype(BF16)

    def dense(jj):
        rows = slice(jj * ATT_BLOCK, (jj + 1) * ATT_BLOCK)
        y_att = jnp.dot(att_scr[rows, :], watt_ref[...], preferred_element_type=F32)
        merged = (gr_ref[0, rows, :].astype(F32) * yrec_scr[rows, :]
                  + gt_ref[0, rows, :].astype(F32) * y_att)
        z = jnp.dot(merged.astype(BF16), wout_ref[...], preferred_element_type=F32)
        r = DEEPNORM_ALPHA * x_ref[0, rows, :] + (1.0 + ga_ref[0]) * z
        x1 = _layer_norm(r, lg_ref[...], lb_ref[...])
        x1_o[0, rows, :] = x1
        u2_o[0, rows, :] = (x1 * (1.0 + sc_ref[0]) + sh_ref[0]).astype(BF16)

    yrec_scr[...] = jnp.dot(orec_ref[0], wrec_ref[...], preferred_element_type=F32)
    attend_finish(0, *attend_scores(0))
    for jj in range(n_blocks):
        if jj + 1 < n_blocks:
            kb, scs = attend_scores(jj + 1)
        dense(jj)
        if jj + 1 < n_blocks:
            attend_finish(jj + 1, kb, scs)


def _mix(x, ga1, sc2, sh2, aq, ak, av, orec, g_rec, g_att, sink, w_rec, w_att, w_out, ln_g, ln_b, tq):
    B, S, D = x.shape
    bias = _att_bias(S)
    tok = lambda b, t: (b, t, 0)
    per_b = lambda b, t: (b, 0, 0)
    c2 = lambda b, t: (0, 0)
    full = lambda a: pl.BlockSpec(a.shape, (lambda b, t: (0,) * a.ndim))
    return pl.pallas_call(
        _mix_kernel,
        grid=(B, S // tq),
        in_specs=[pl.BlockSpec((1, tq, D), tok),
                  pl.BlockSpec((1, 1, D), per_b), pl.BlockSpec((1, 1, D), per_b),
                  pl.BlockSpec((1, 1, D), per_b),
                  pl.BlockSpec((1, tq, ATT_QWIDTH), tok),
                  pl.BlockSpec((1, S, ATT_KVWIDTH), per_b),
                  pl.BlockSpec((1, S, ATT_KVWIDTH), per_b),
                  pl.BlockSpec((1, tq, REC_WIDTH), tok),
                  pl.BlockSpec((1, tq, D), tok), pl.BlockSpec((1, tq, D), tok),
                  pl.BlockSpec(memory_space=pltpu.SMEM),
                  full(bias), full(w_rec), full(w_att), full(w_out),
                  pl.BlockSpec((1, D), c2), pl.BlockSpec((1, D), c2)],
        out_specs=[pl.BlockSpec((1, tq, D), tok), pl.BlockSpec((1, tq, D), tok)],
        out_shape=[jax.ShapeDtypeStruct((B, S, D), F32), jax.ShapeDtypeStruct((B, S, D), BF16)],
        scratch_shapes=[pltpu.VMEM((ATT_KV_HEADS, S // ATT_BLOCK, 2, ATT_BLOCK, LANES), BF16),
                        pltpu.VMEM((ATT_KV_HEADS, S // ATT_BLOCK, 2, ATT_BLOCK, 2 * LANES), BF16),
                        pltpu.VMEM((tq, ATT_QWIDTH), BF16),
                        pltpu.VMEM((tq, D), F32)],
        compiler_params=pltpu.CompilerParams(
            dimension_semantics=("arbitrary", "arbitrary"), vmem_limit_bytes=VMEM_LIMIT),
        name="mix",
    )(x, ga1, sc2, sh2, aq, ak, av, orec, g_rec, g_att, sink.reshape(1, ATT_Q_HEADS), bias,
      w_rec, w_att, w_out, ln_g.reshape(1, D), ln_b.reshape(1, D))


HALO = 16
FFN_CHUNK = 256
FFN_ROWS = 128


def _gelu_tanh(x):
    return 0.5 * x * (1.0 + jnp.tanh(np.sqrt(2.0 / np.pi).astype(np.float32) * (x + 0.044715 * (x * x * x))))


def _ffn_kernel(u_ref, up_ref, un_ref, x1_ref, ga_ref, wup_ref, cw_ref, cb_ref, wdn_ref,
                lg_ref, lb_ref, o_ref, ucat, h_scr, acc_ref):
    t = pl.program_id(1)
    nt_ = pl.num_programs(1)
    tm = u_ref.shape[1]
    d_ff = wdn_ref.shape[0]
    ucat[0:HALO] = jnp.where(t > 0, up_ref[0], jnp.zeros_like(up_ref[0]))
    ucat[HALO:HALO + tm] = u_ref[0]
    ucat[HALO + tm:] = jnp.where(t < nt_ - 1, un_ref[0], jnp.zeros_like(un_ref[0]))
    u = ucat[...]

    n_chunks = d_ff // FFN_CHUNK

    def up(j, part):
        if j < n_chunks:
            col = part * d_ff + j * FFN_CHUNK
            h = jnp.dot(u, wup_ref[:, col:col + FFN_CHUNK], preferred_element_type=F32)
            for s in range(FFN_CHUNK // LANES):
                h_scr[j % 2, part, s] = h[:, s * LANES:(s + 1) * LANES]

    def conv(j, part, r0):
        slabs = []
        for s in range(FFN_CHUNK // LANES):
            col = part * d_ff + j * FFN_CHUNK + s * LANES
            w = cw_ref[:, col:col + LANES]
            out = cb_ref[:, col:col + LANES]
            for tap in range(CONV_K):
                lo = HALO + tap - CONV_K // 2 + r0
                out = out + h_scr[j % 2, part, s, lo:lo + FFN_ROWS, :] * w[tap:tap + 1]
            slabs.append(out)
        return jnp.concatenate(slabs, axis=1)

    def down(j, r0):
        rows = slice(r0, r0 + FFN_ROWS)
        a = (_gelu_tanh(conv(j, 1, r0)) * conv(j, 0, r0)).astype(BF16)
        part = jnp.dot(a, wdn_ref[j * FFN_CHUNK:(j + 1) * FFN_CHUNK, :], preferred_element_type=F32)
        if j == 0:
            acc_ref[rows, :] = part
        elif j < n_chunks - 1:
            acc_ref[rows, :] += part
        else:
            r = DEEPNORM_ALPHA * x1_ref[0, rows, :] + (1.0 + ga_ref[0]) * (acc_ref[rows, :] + part)
            o_ref[0, rows, :] = _layer_norm(r, lg_ref[...], lb_ref[...])

    up(0, 0)
    up(0, 1)
    row_blocks = list(range(0, tm, FFN_ROWS))
    half = len(row_blocks) // 2
    for j in range(n_chunks):
        up(j + 1, 0)
        for r0 in row_blocks[:half]:
            down(j, r0)
        up(j + 1, 1)
        for r0 in row_blocks[half:]:
            down(j, r0)


def _ffn(u2, x1, ga2, w_up, conv_w, conv_b, w_down, ln_g, ln_b, tm):
    B, S, D = x1.shape
    d_ff = w_down.shape[0]
    assert d_ff % FFN_CHUNK == 0 and tm % HALO == 0
    nh = tm // HALO
    last = S // HALO - 1
    tok = lambda b, t: (b, t, 0)
    per_b = lambda b, t: (b, 0, 0)
    c2 = lambda b, t: (0, 0)
    return pl.pallas_call(
        _ffn_kernel,
        grid=(B, S // tm),
        in_specs=[pl.BlockSpec((1, tm, D), tok),
                  pl.BlockSpec((1, HALO, D), lambda b, t: (b, jnp.maximum(t * nh - 1, 0), 0)),
                  pl.BlockSpec((1, HALO, D), lambda b, t: (b, jnp.minimum((t + 1) * nh, last), 0)),
                  pl.BlockSpec((1, tm, D), tok),
                  pl.BlockSpec((1, 1, D), per_b),
                  pl.BlockSpec(w_up.shape, c2, pipeline_mode=pl.Buffered(1)),
                  pl.BlockSpec(conv_w.shape, c2),
                  pl.BlockSpec((1, 2 * d_ff), c2),
                  pl.BlockSpec(w_down.shape, c2, pipeline_mode=pl.Buffered(1)),
                  pl.BlockSpec((1, D), c2), pl.BlockSpec((1, D), c2)],
        out_specs=pl.BlockSpec((1, tm, D), tok),
        out_shape=jax.ShapeDtypeStruct((B, S, D), F32),
        scratch_shapes=[pltpu.VMEM((tm + 2 * HALO, D), BF16),
                        pltpu.VMEM((2, 2, FFN_CHUNK // LANES, tm + 2 * HALO, LANES), F32),
                        pltpu.VMEM((tm, D), F32)],
        compiler_params=pltpu.CompilerParams(vmem_limit_bytes=VMEM_LIMIT),
        name="ffn",
    )(u2, u2, u2, x1, ga2, w_up, conv_w, conv_b.reshape(1, 2 * d_ff), w_down,
      ln_g.reshape(1, D), ln_b.reshape(1, D))


def kernel(x, c, positions, w_ada, b_ada, w_in, rec_lower_bound, rec_norm_g, w_rec_branch,
           attn_sink, w_attn_branch, w_out, ln1_g, ln1_b, w_up, conv_w, conv_b, w_down,
           ln2_g, ln2_b):
    B, S, D = x.shape
    assert w_ada.shape[0] == DEPTH and rec_lower_bound.shape[1] == DEPTH + 1
    tm_proj = min(1024, S)
    tq = min(1024, S)
    tm_ffn = min(1024, S)
    for l in range(DEPTH):
        mods = _ada(c, w_ada[l], b_ada[l])
        sh1, sc1, ga1, sh2, sc2, ga2 = [m.reshape(B, 1, D) for m in jnp.split(mods, 6, axis=-1)]
        (q, kf, kb, lf, lb, v, g, aq, ak, av, g_rec, g_att) = _proj(
            x, sc1, sh1, positions, w_in[l].astype(BF16), rec_lower_bound, tm_proj)
        orec = _rec(q, kf, kb, lf, lb, v, g, rec_norm_g[l])
        x1, u2 = _mix(x, ga1, sc2, sh2, aq, ak, av, orec, g_rec, g_att, attn_sink[l],
                      w_rec_branch[l].astype(BF16), w_attn_branch[l].astype(BF16),
                      w_out[l].astype(BF16), ln1_g[l], ln1_b[l], tq)
        x = _ffn(u2, x1, ga2, w_up[l].astype(BF16), conv_w[l], conv_b[l], w_down[l].astype(BF16),
                 ln2_g[l], ln2_b[l], tm_ffn)
    return x
```

```python
import numpy as np
import jax
import jax.numpy as jnp
from jax import lax
from jax.experimental import pallas as pl
from jax.experimental.pallas import tpu as pltpu

F32 = jnp.float32
BF16 = jnp.bfloat16

REC_HEADS = 4
REC_DK = 128
REC_DV = 128
REC_WIDTH = REC_HEADS * REC_DK
ATT_Q_HEADS = 8
ATT_KV_HEADS = 2
ATT_HEAD_DIM = 64
ATT_QWIDTH = ATT_Q_HEADS * ATT_HEAD_DIM
ATT_KVWIDTH = ATT_KV_HEADS * ATT_HEAD_DIM
ATT_WINDOW = 128
ATT_BLOCK = 128
ROPE_THETA = 10000.0
CONV_K = 3
LN_EPS = 1e-5
RMS_EPS = 1e-6
DEPTH = 1
DEEPNORM_ALPHA = (2 * DEPTH) ** 0.25

LANES = 128
REC_CHUNK = 64
REC_LEVELS = (32, 16, 8, 4, 2, 1)
REC_MXU_LEVELS = (4, 2, 1)
LOG2E = 1.4426950408889634
NEG_BIG = -1e30
VMEM_LIMIT = 56 * 1024 * 1024


def _sigmoid(x):
    return 1.0 / (1.0 + jnp.exp(-x))


def _silu(x):
    return x * _sigmoid(x)


def _layer_norm(r, g, b):
    mu = jnp.mean(r, axis=-1, keepdims=True)
    rc = r - mu
    var = jnp.mean(rc * rc, axis=-1, keepdims=True)
    return rc * lax.rsqrt(var + LN_EPS) * g + b


def _ada_kernel(c_ref, w_ref, b_ref, o_ref):
    c = c_ref[...]
    o_ref[...] = jnp.dot(_silu(c), w_ref[...], preferred_element_type=F32) + b_ref[...]


def _ada(c, w_ada, b_ada):
    B, D = c.shape
    N = w_ada.shape[1]
    return pl.pallas_call(
        _ada_kernel,
        grid=(N // D,),
        in_specs=[pl.BlockSpec((B, D), lambda j: (0, 0)),
                  pl.BlockSpec((D, D), lambda j: (0, j)),
                  pl.BlockSpec((1, D), lambda j: (0, j))],
        out_specs=pl.BlockSpec((B, D), lambda j: (0, j)),
        out_shape=jax.ShapeDtypeStruct((B, N), F32),
        name="ada",
    )(c, w_ada, b_ada.reshape(1, N))


PROJ_COLS = 256


def _proj_kernel(x_ref, sc_ref, sh_ref, pos_ref, invf_ref, w_ref, rlb_ref,
                 q_o, kf_o, kb_o, lf_o, lb_o, v_o, g_o, aq_o, ak_o, av_o, gr_o, ga_o):
    u = (x_ref[0] * (1.0 + sc_ref[0]) + sh_ref[0]).astype(BF16)
    D = x_ref.shape[-1]
    W = REC_WIDTH

    def pieces(base, width):
        for c in range(0, width, PROJ_COLS):
            yield slice(c, c + PROJ_COLS), jnp.dot(
                u, w_ref[:, base + c:base + c + PROJ_COLS], preferred_element_type=F32)

    for sl, y in pieces(0, W):
        q_o[0, :, sl] = (_silu(y) * (REC_DK ** -0.5)).astype(BF16)

    for direction, (k_o, l_o) in enumerate(((kf_o, lf_o), (kb_o, lb_o))):
        r = rlb_ref[direction]
        e = jnp.exp(r - jnp.max(r, axis=0, keepdims=True))
        lbd = e[0:1] / jnp.sum(e, axis=0, keepdims=True)
        for sl, y in pieces((1 + direction) * W, W):
            f = lbd[:, sl] + (1.0 - lbd[:, sl]) * _sigmoid(y)
            k_o[0, :, sl] = (1.0 - f).astype(BF16)
            l_o[0, :, sl] = jnp.log(f)

    for sl, y in pieces(3 * W, W):
        v_o[0, :, sl] = y.astype(BF16)
    for sl, y in pieces(4 * W, W):
        g_o[0, :, sl] = _silu(y).astype(BF16)

    quarter = LANES // 4
    ang_c = pos_ref[0].astype(F32) * invf_ref[...]
    lane = lax.broadcasted_iota(jnp.int32, (x_ref.shape[1], LANES), 1)
    rowm = lax.broadcasted_iota(jnp.int32, (x_ref.shape[1], LANES), 0) % 4

    def expand(tc):
        rep = jnp.repeat(tc, 4, axis=0)
        y = rep
        for j in range(1, 4):
            y = jnp.where(rowm == j, pltpu.roll(rep, LANES - j * quarter, axis=1), y)
        z = jnp.where(lane < quarter, y, 0.0)
        z = z + pltpu.roll(z, quarter, axis=1)
        return z + pltpu.roll(z, 2 * quarter, axis=1)

    cos = expand(jnp.cos(ang_c))
    sin = expand(jnp.sin(ang_c))
    first_half = (lane % ATT_HEAD_DIM) < (ATT_HEAD_DIM // 2)
    sin_signed = jnp.where(first_half, -sin, sin)

    def rope(t):
        partner = jnp.where(first_half,
                            pltpu.roll(t, LANES - ATT_HEAD_DIM // 2, axis=1),
                            pltpu.roll(t, ATT_HEAD_DIM // 2, axis=1))
        return t * cos + partner * sin_signed

    c0 = 5 * W
    for sl, y in pieces(c0, ATT_QWIDTH):
        for s in range(PROJ_COLS // LANES):
            lanes = slice(sl.start + s * LANES, sl.start + (s + 1) * LANES)
            aq_o[0, :, lanes] = (rope(y[:, s * LANES:(s + 1) * LANES])
                                 * (ATT_HEAD_DIM ** -0.5 * LOG2E)).astype(BF16)
    c0 += ATT_QWIDTH
    assert 2 * ATT_KVWIDTH == PROJ_COLS
    for _, y in pieces(c0, 2 * ATT_KVWIDTH):
        ak_o[0] = rope(y[:, :ATT_KVWIDTH]).astype(BF16)
        av_o[0] = y[:, ATT_KVWIDTH:].astype(BF16)
    c0 += 2 * ATT_KVWIDTH
    for sl, y in pieces(c0, D):
        gr_o[0, :, sl] = _sigmoid(y).astype(BF16)
    for sl, y in pieces(c0 + D, D):
        ga_o[0, :, sl] = _sigmoid(y).astype(BF16)


def _proj(x, sc1, sh1, positions, w_in_bf, rec_lower_bound, tm):
    B, S, D = x.shape
    d_in = w_in_bf.shape[1]
    half = ATT_HEAD_DIM // 2
    inv_freq = ROPE_THETA ** (-jnp.arange(half, dtype=F32) / half)
    invf = jnp.tile(inv_freq, LANES // half).reshape(1, LANES)
    W = REC_WIDTH
    tok = lambda b, t: (b, t, 0)
    per_b = lambda b, t: (b, 0, 0)
    outs = [(W, BF16), (W, BF16), (W, BF16), (W, F32), (W, F32), (W, BF16), (W, BF16),
            (ATT_QWIDTH, BF16), (ATT_KVWIDTH, BF16), (ATT_KVWIDTH, BF16), (D, BF16), (D, BF16)]
    return pl.pallas_call(
        _proj_kernel,
        grid=(B, S // tm),
        in_specs=[pl.BlockSpec((1, tm, D), tok),
                  pl.BlockSpec((1, 1, D), per_b),
                  pl.BlockSpec((1, 1, D), per_b),
                  pl.BlockSpec((1, tm // 4, LANES), tok),
                  pl.BlockSpec((1, LANES), lambda b, t: (0, 0)),
                  pl.BlockSpec((D, d_in), lambda b, t: (0, 0), pipeline_mode=pl.Buffered(1)),
                  pl.BlockSpec(rec_lower_bound.shape, lambda b, t: (0, 0, 0))],
        out_specs=[pl.BlockSpec((1, tm, w), tok) for w, _ in outs],
        out_shape=[jax.ShapeDtypeStruct((B, S, w), dt) for w, dt in outs],
        compiler_params=pltpu.CompilerParams(vmem_limit_bytes=VMEM_LIMIT),
        name="proj",
    )(x, sc1, sh1, jnp.repeat(positions.reshape(B, S // 4, 4), LANES // 4, axis=-1), invf, w_in_bf,
      rec_lower_bound)


def _rec_constants():
    C = REC_CHUNK
    nblk = 1 + len(REC_MXU_LEVELS)
    mf = np.zeros((nblk, C, C), np.float32)
    r = np.arange(C)
    for t in range(C):
        mf[0, t] = r <= t
        for i, m in enumerate(REC_MXU_LEVELS):
            boundary = (t // (2 * m)) * 2 * m + m
            if t >= boundary:
                mf[1 + i, t] = (r >= boundary) & (r <= t)
            else:
                mf[1 + i, t] = (r > t) & (r <= boundary - 1)
    mb = mf[:, ::-1, ::-1]
    x = r[:, None] ^ r[None, :]
    lev = np.where(x > 0, np.floor(np.log2(np.maximum(x, 1))).astype(np.int32), -1)
    invalid = -2
    levf = np.where(r[:, None] >= r[None, :], lev, invalid).astype(np.int32)
    levb = np.where(r[:, None] <= r[None, :], lev, invalid).astype(np.int32)
    m3 = lambda m: np.concatenate([m.reshape(nblk * C, C)] * 3, axis=1)
    return (jnp.asarray(np.stack([m3(mf), m3(mb)]), BF16),
            jnp.asarray(np.stack([levf, levb])))


REC_HEADS_PER_STEP = 2
REC_UNROLL = 2
REC_STATE_UNROLL = 32


def _rec_kernel(q_ref, kf_ref, kb_ref, lf_ref, lb_ref, v_ref, g_ref, ng_ref, m_ref, lev_ref,
                o_ref, acc_ref, qt_ref, upd_ref, dec_ref):
    C = REC_CHUNK
    HP = REC_HEADS_PER_STEP
    S = q_ref.shape[1]
    n_chunks = S // C
    row = lax.broadcasted_iota(jnp.int32, (C, LANES), 0)
    nt = (((1,), (1,)), ((), ()))
    tn = (((0,), (0,)), ((), ()))
    k_refs = (kf_ref, kb_ref)
    l_refs = (lf_ref, lb_ref)

    def local(i, carry):
        U = REC_UNROLL
        units = [(c, d, h) for c in range(U) for d in range(2) for h in range(HP)]
        ns = {(c, d): (U * i + c if d == 0 else n_chunks - 1 - (U * i + c))
              for c in range(U) for d in range(2)}
        rows = {cd: pl.ds(pl.multiple_of(n * C, C), C) for cd, n in ns.items()}
        sls = [slice(h * LANES, (h + 1) * LANES) for h in range(HP)]
        xs = {}
        for cd in ns:
            lg = l_refs[cd[1]][0, rows[cd], :]
            hi = lg.astype(BF16)
            r1 = lg - hi.astype(F32)
            mid = r1.astype(BF16)
            lo = (r1 - mid.astype(F32)).astype(BF16)
            e = jnp.dot(m_ref[cd[1]], jnp.concatenate([hi, mid, lo], axis=0),
                        preferred_element_type=F32)
            d = cd[1]
            b = e[0:C]
            last = C - 1 if d == 0 else 0
            ex = [b, b[last:last + 1] - b]
            for m in REC_LEVELS:
                if m in REC_MXU_LEVELS:
                    j = REC_MXU_LEVELS.index(m)
                    ex.append(e[(1 + j) * C:(2 + j) * C])
                    continue
                parts = []
                for r0 in range(0, C, 8):
                    base = (r0 // (2 * m)) * 2 * m + m
                    ref_row = base - 1 if d == 0 else base
                    q_side = ((r0 & m) != 0) if d == 0 else ((r0 & m) == 0)
                    ref = b[ref_row:ref_row + 1]
                    parts.append(b[r0:r0 + 8] - ref if q_side else ref - b[r0:r0 + 8])
                ex.append(jnp.concatenate(parts, axis=0))
            xs[cd] = jnp.exp(jnp.concatenate(ex, axis=0))
        ws, kts, qs, ks, vs = {}, {}, {}, {}, {}
        for c, d in ns:
            cd = (c, d)
            last = C - 1 if d == 0 else 0
            dec_ref[d, pl.ds(ns[cd], 1), :] = xs[cd][last:last + 1]
            for h in range(HP):
                un = (c, d, h)
                q = q_ref[0, rows[cd], sls[h]]
                k = k_refs[d][0, rows[cd], sls[h]]
                qs[un], ks[un], vs[un] = q, k, v_ref[0, rows[cd], sls[h]]
                qf = q.astype(F32)
                kf = k.astype(F32)
                qt_ref[d, rows[cd], sls[h]] = (qf * xs[cd][0:C, sls[h]]).astype(BF16)
                kts[un] = (kf * xs[cd][C:2 * C, sls[h]]).astype(BF16)
                for j, m in enumerate(REC_LEVELS):
                    q_side = ((row & m) != 0) if d == 0 else ((row & m) == 0)
                    ws[un, j] = (jnp.where(q_side, qf, kf)
                                 * xs[cd][(2 + j) * C:(3 + j) * C, sls[h]]).astype(BF16)
        als = {}
        for un in units:
            als[un, -1] = lax.dot_general(qs[un], ks[un], nt, preferred_element_type=F32)
            for j in range(len(REC_LEVELS)):
                als[un, j] = lax.dot_general(ws[un, j], ws[un, j], nt, preferred_element_type=F32)
        for un in units:
            c, d, h = un
            upd_ref[d, h, ns[c, d]] = lax.dot_general(vs[un], kts[un], tn,
                                                      preferred_element_type=F32)
        avs = {}
        for un in units:
            lev = lev_ref[un[1]]
            a = jnp.where(lev == -1, als[un, -1], 0.0)
            for j, m in enumerate(REC_LEVELS):
                a = jnp.where(lev == (m.bit_length() - 1), als[un, j], a)
            avs[un] = a.astype(BF16)
        for un in units:
            c, d, h = un
            acc_ref[d, rows[c, d], sls[h]] = jnp.dot(avs[un], vs[un], preferred_element_type=F32)
        return carry

    lax.fori_loop(0, n_chunks // REC_UNROLL, local, 0)

    for direction in range(2):
        def carried(i, sts, direction=direction):
            n = i if direction == 0 else n_chunks - 1 - i
            rows = pl.ds(pl.multiple_of(n * C, C), C)
            d = dec_ref[direction, pl.ds(n, 1), :]
            out = []
            for h in range(HP):
                sl = slice(h * LANES, (h + 1) * LANES)
                st = sts[h]
                acc_ref[direction, rows, sl] += lax.dot_general(
                    qt_ref[direction, rows, sl], st.astype(BF16), nt, preferred_element_type=F32)
                out.append(st * d[:, sl] + upd_ref[direction, h, n])
            return tuple(out)

        lax.fori_loop(0, n_chunks, carried,
                      tuple(jnp.zeros((REC_DV, REC_DK), F32) for _ in range(HP)),
                      unroll=REC_STATE_UNROLL)

    RB = 256

    def finish(i, carry):
        rows = pl.ds(pl.multiple_of(i * RB, RB), RB)
        for h in range(HP):
            sl = slice(h * LANES, (h + 1) * LANES)
            o = acc_ref[0, rows, sl] + acc_ref[1, rows, sl]
            o = o * lax.rsqrt(jnp.mean(o * o, axis=-1, keepdims=True) + RMS_EPS) * ng_ref[...]
            o_ref[0, rows, sl] = (o * g_ref[0, rows, sl].astype(F32)).astype(BF16)
        return carry

    lax.fori_loop(0, S // RB, finish, 0, unroll=8)


def _rec(q, kf, kb, lf, lb, v, g, norm_g):
    B, S, W = q.shape
    HP = REC_HEADS_PER_STEP
    m3, lev = _rec_constants()
    n_chunks = S // REC_CHUNK
    head = lambda b, h: (b, 0, h)
    blk = pl.BlockSpec((1, S, HP * LANES), head)
    return pl.pallas_call(
        _rec_kernel,
        grid=(B, REC_HEADS // HP),
        in_specs=[blk, blk, blk, blk, blk, blk, blk,
                  pl.BlockSpec((1, REC_DV), lambda b, h: (0, 0)),
                  pl.BlockSpec(m3.shape, lambda b, h: (0, 0, 0)),
                  pl.BlockSpec(lev.shape, lambda b, h: (0, 0, 0))],
        out_specs=blk,
        out_shape=jax.ShapeDtypeStruct((B, S, W), BF16),
        scratch_shapes=[pltpu.VMEM((2, S, HP * REC_DV), F32),
                        pltpu.VMEM((2, S, HP * REC_DK), BF16),
                        pltpu.VMEM((2, HP, n_chunks, REC_DV, REC_DK), F32),
                        pltpu.VMEM((2, n_chunks, HP * REC_DK), F32)],
        compiler_params=pltpu.CompilerParams(vmem_limit_bytes=VMEM_LIMIT),
        name="rec",
    )(q, kf, kb, lf, lb, v, g, norm_g.reshape(1, REC_DV), m3, lev)


def _att_bias(S):
    rowi = np.arange(ATT_BLOCK)[:, None]
    coli = np.arange(3 * ATT_BLOCK)[None, :]
    out = []
    for off in (0, ATT_BLOCK, 2 * ATT_BLOCK):
        ok = np.abs(coli - off - rowi) <= ATT_WINDOW
        m = np.where(ok, 0.0, NEG_BIG).reshape(ATT_BLOCK, 3, 1, ATT_BLOCK)
        out.append(np.broadcast_to(m, (ATT_BLOCK, 3, 2, ATT_BLOCK)).reshape(ATT_BLOCK, 6 * ATT_BLOCK))
    return jnp.asarray(np.stack(out), F32)


def _mix_kernel(x_ref, ga_ref, sc_ref, sh_ref, aq_ref, ak_ref, av_ref, orec_ref, gr_ref, gt_ref,
                sink_ref, bias_ref, wrec_ref, watt_ref, wout_ref, lg_ref, lb_ref,
                x1_o, u2_o, kvar, vvar, att_scr, yrec_scr):
    t = pl.program_id(1)
    S = ak_ref.shape[1]
    TQ = x_ref.shape[1]
    nb = S // ATT_BLOCK
    KW = 3 * ATT_BLOCK
    HALF = LANES // 2
    nt = (((1,), (1,)), ((), ()))

    @pl.when(t == 0)
    def _():
        lane = lax.broadcasted_iota(jnp.int32, (S, LANES), 1)
        lo = lane < HALF
        k = ak_ref[0].astype(F32)
        v = av_ref[0].astype(F32)
        ksw = pltpu.roll(k, HALF, axis=1)
        vsw = pltpu.roll(v, HALF, axis=1)
        zero = jnp.zeros_like(k)
        for i, (kk, vv) in enumerate([(jnp.where(lo, k, zero), jnp.where(lo, v, zero)),
                                      (jnp.where(lo, zero, ksw), jnp.where(lo, zero, vsw)),
                                      (jnp.where(lo, ksw, zero), jnp.where(lo, vsw, zero)),
                                      (jnp.where(lo, zero, k), jnp.where(lo, zero, v))]):
            ones = jnp.where(lo, 1.0, 0.0) if i % 2 == 0 else jnp.where(lo, 0.0, 1.0)
            vo = jnp.concatenate([vv, ones], axis=1).astype(BF16)
            kvar[i // 2, :, i % 2] = kk.astype(BF16).reshape(nb, ATT_BLOCK, LANES)
            vvar[i // 2, :, i % 2] = vo.reshape(nb, ATT_BLOCK, 2 * LANES)

    lane_q = lax.broadcasted_iota(jnp.int32, (ATT_BLOCK, LANES), 1)
    lo_q = lane_q < HALF
    n_blocks = TQ // ATT_BLOCK

    def attend_scores(jj):
        rows = slice(jj * ATT_BLOCK, (jj + 1) * ATT_BLOCK)
        j = t * n_blocks + jj
        kb = jnp.clip(j - 1, 0, nb - 3)
        bidx = jnp.where(j == 0, 0, jnp.where(j == nb - 1, 2, 1))
        bias = bias_ref[bidx]
        scs = {}
        for s in range(ATT_QWIDTH // LANES):
            hk = (2 * s) // (ATT_Q_HEADS // ATT_KV_HEADS)
            qs = aq_ref[0, rows, s * LANES:(s + 1) * LANES]
            kwin = kvar[hk, pl.ds(kb, 3)].reshape(2 * KW, LANES)
            scs[s] = lax.dot_general(qs, kwin, nt, preferred_element_type=F32) + bias
        return kb, scs

    def attend_finish(jj, kb, scs):
        rows = slice(jj * ATT_BLOCK, (jj + 1) * ATT_BLOCK)
        for s in range(ATT_QWIDTH // LANES):
            hk = (2 * s) // (ATT_Q_HEADS // ATT_KV_HEADS)
            sc = scs[s]
            blocks = [sc[:, i * ATT_BLOCK:(i + 1) * ATT_BLOCK] for i in range(6)]
            ps, den_sink = list(blocks), []
            for par in range(2):
                sk = sink_ref[0, 2 * s + par] * LOG2E
                m3 = jnp.maximum(jnp.maximum(blocks[par], blocks[2 + par]), blocks[4 + par])
                mx = jnp.maximum(jnp.max(m3, axis=-1, keepdims=True), sk)
                for i in range(par, 6, 2):
                    ps[i] = jnp.exp2(blocks[i] - mx).astype(BF16)
                den_sink.append(jnp.exp2(sk - mx))
            vwin = vvar[hk, pl.ds(kb, 3)].reshape(2 * KW, 2 * LANES)
            acc = jnp.dot(jnp.concatenate(ps, axis=1), vwin, preferred_element_type=F32)
            den = acc[:, LANES:] + jnp.where(lo_q, den_sink[0], den_sink[1])
            att_scr[rows, s * LANES:(s + 1) * LANES] = (acc[:, :LANES] / den).astype(BF16)

    def dense(jj):
        rows = slice(jj * ATT_BLOCK, (jj + 1) * ATT_BLOCK)
        y_att = jnp.dot(att_scr[rows, :], watt_ref[...], preferred_element_type=F32)
        merged = (gr_ref[0, rows, :].astype(F32) * yrec_scr[rows, :]
                  + gt_ref[0, rows, :].astype(F32) * y_att)
        z = jnp.dot(merged.astype(BF16), wout_ref[...], preferred_element_type=F32)
        r = DEEPNORM_ALPHA * x_ref[0, rows, :] + (1.0 + ga_ref[0]) * z
        x1 = _layer_norm(r, lg_ref[...], lb_ref[...])
        x1_o[0, rows, :] = x1
        u2_o[0, rows, :] = (x1 * (1.0 + sc_ref[0]) + sh_ref[0]).astype(BF16)

    yrec_scr[...] = jnp.dot(orec_ref[0], wrec_ref[...], preferred_element_type=F32)
    attend_finish(0, *attend_scores(0))
    for jj in range(n_blocks):
        if jj + 1 < n_blocks:
            kb, scs = attend_scores(jj + 1)
        dense(jj)
        if jj + 1 < n_blocks:
            attend_finish(jj + 1, kb, scs)


def _mix(x, ga1, sc2, sh2, aq, ak, av, orec, g_rec, g_att, sink, w_rec, w_att, w_out, ln_g, ln_b, tq):
    B, S, D = x.shape
    bias = _att_bias(S)
    tok = lambda b, t: (b, t, 0)
    per_b = lambda b, t: (b, 0, 0)
    c2 = lambda b, t: (0, 0)
    full = lambda a: pl.BlockSpec(a.shape, (lambda b, t: (0,) * a.ndim))
    return pl.pallas_call(
        _mix_kernel,
        grid=(B, S // tq),
        in_specs=[pl.BlockSpec((1, tq, D), tok),
                  pl.BlockSpec((1, 1, D), per_b), pl.BlockSpec((1, 1, D), per_b),
                  pl.BlockSpec((1, 1, D), per_b),
                  pl.BlockSpec((1, tq, ATT_QWIDTH), tok),
                  pl.BlockSpec((1, S, ATT_KVWIDTH), per_b),
                  pl.BlockSpec((1, S, ATT_KVWIDTH), per_b),
                  pl.BlockSpec((1, tq, REC_WIDTH), tok),
                  pl.BlockSpec((1, tq, D), tok), pl.BlockSpec((1, tq, D), tok),
                  pl.BlockSpec(memory_space=pltpu.SMEM),
                  full(bias), full(w_rec), full(w_att), full(w_out),
                  pl.BlockSpec((1, D), c2), pl.BlockSpec((1, D), c2)],
        out_specs=[pl.BlockSpec((1, tq, D), tok), pl.BlockSpec((1, tq, D), tok)],
        out_shape=[jax.ShapeDtypeStruct((B, S, D), F32), jax.ShapeDtypeStruct((B, S, D), BF16)],
        scratch_shapes=[pltpu.VMEM((ATT_KV_HEADS, S // ATT_BLOCK, 2, ATT_BLOCK, LANES), BF16),
                        pltpu.VMEM((ATT_KV_HEADS, S // ATT_BLOCK, 2, ATT_BLOCK, 2 * LANES), BF16),
                        pltpu.VMEM((tq, ATT_QWIDTH), BF16),
                        pltpu.VMEM((tq, D), F32)],
        compiler_params=pltpu.CompilerParams(
            dimension_semantics=("arbitrary", "arbitrary"), vmem_limit_bytes=VMEM_LIMIT),
        name="mix",
    )(x, ga1, sc2, sh2, aq, ak, av, orec, g_rec, g_att, sink.reshape(1, ATT_Q_HEADS), bias,
      w_rec, w_att, w_out, ln_g.reshape(1, D), ln_b.reshape(1, D))


HALO = 16
FFN_CHUNK = 256
FFN_ROWS = 128


def _gelu_tanh(x):
    return 0.5 * x * (1.0 + jnp.tanh(np.sqrt(2.0 / np.pi).astype(np.float32) * (x + 0.044715 * (x * x * x))))


def _ffn_kernel(u_ref, up_ref, un_ref, x1_ref, ga_ref, wup_ref, cw_ref, cb_ref, wdn_ref,
                lg_ref, lb_ref, o_ref, ucat, h_scr, acc_ref):
    t = pl.program_id(1)
    nt_ = pl.num_programs(1)
    tm = u_ref.shape[1]
    d_ff = wdn_ref.shape[0]
    ucat[0:HALO] = jnp.where(t > 0, up_ref[0], jnp.zeros_like(up_ref[0]))
    ucat[HALO:HALO + tm] = u_ref[0]
    ucat[HALO + tm:] = jnp.where(t < nt_ - 1, un_ref[0], jnp.zeros_like(un_ref[0]))
    u = ucat[...]

    n_chunks = d_ff // FFN_CHUNK

    def up(j, part):
        if j < n_chunks:
            col = part * d_ff + j * FFN_CHUNK
            h = jnp.dot(u, wup_ref[:, col:col + FFN_CHUNK], preferred_element_type=F32)
            for s in range(FFN_CHUNK // LANES):
                h_scr[j % 2, part, s] = h[:, s * LANES:(s + 1) * LANES]

    def conv(j, part, r0):
        slabs = []
        for s in range(FFN_CHUNK // LANES):
            col = part * d_ff + j * FFN_CHUNK + s * LANES
            w = cw_ref[:, col:col + LANES]
            out = cb_ref[:, col:col + LANES]
            for tap in range(CONV_K):
                lo = HALO + tap - CONV_K // 2 + r0
                out = out + h_scr[j % 2, part, s, lo:lo + FFN_ROWS, :] * w[tap:tap + 1]
            slabs.append(out)
        return jnp.concatenate(slabs, axis=1)

    def down(j, r0):
        rows = slice(r0, r0 + FFN_ROWS)
        a = (_gelu_tanh(conv(j, 1, r0)) * conv(j, 0, r0)).astype(BF16)
        part = jnp.dot(a, wdn_ref[j * FFN_CHUNK:(j + 1) * FFN_CHUNK, :], preferred_element_type=F32)
        if j == 0:
            acc_ref[rows, :] = part
        elif j < n_chunks - 1:
            acc_ref[rows, :] += part
        else:
            r = DEEPNORM_ALPHA * x1_ref[0, rows, :] + (1.0 + ga_ref[0]) * (acc_ref[rows, :] + part)
            o_ref[0, rows, :] = _layer_norm(r, lg_ref[...], lb_ref[...])

    up(0, 0)
    up(0, 1)
    row_blocks = list(range(0, tm, FFN_ROWS))
    half = len(row_blocks) // 2
    for j in range(n_chunks):
        up(j + 1, 0)
        for r0 in row_blocks[:half]:
            down(j, r0)
        up(j + 1, 1)
        for r0 in row_blocks[half:]:
            down(j, r0)


def _ffn(u2, x1, ga2, w_up, conv_w, conv_b, w_down, ln_g, ln_b, tm):
    B, S, D = x1.shape
    d_ff = w_down.shape[0]
    assert d_ff % FFN_CHUNK == 0 and tm % HALO == 0
    nh = tm // HALO
    last = S // HALO - 1
    tok = lambda b, t: (b, t, 0)
    per_b = lambda b, t: (b, 0, 0)
    c2 = lambda b, t: (0, 0)
    return pl.pallas_call(
        _ffn_kernel,
        grid=(B, S // tm),
        in_specs=[pl.BlockSpec((1, tm, D), tok),
                  pl.BlockSpec((1, HALO, D), lambda b, t: (b, jnp.maximum(t * nh - 1, 0), 0)),
                  pl.BlockSpec((1, HALO, D), lambda b, t: (b, jnp.minimum((t + 1) * nh, last), 0)),
                  pl.BlockSpec((1, tm, D), tok),
                  pl.BlockSpec((1, 1, D), per_b),
                  pl.BlockSpec(w_up.shape, c2, pipeline_mode=pl.Buffered(1)),
                  pl.BlockSpec(conv_w.shape, c2),
                  pl.BlockSpec((1, 2 * d_ff), c2),
                  pl.BlockSpec(w_down.shape, c2, pipeline_mode=pl.Buffered(1)),
                  pl.BlockSpec((1, D), c2), pl.BlockSpec((1, D), c2)],
        out_specs=pl.BlockSpec((1, tm, D), tok),
        out_shape=jax.ShapeDtypeStruct((B, S, D), F32),
        scratch_shapes=[pltpu.VMEM((tm + 2 * HALO, D), BF16),
                        pltpu.VMEM((2, 2, FFN_CHUNK // LANES, tm + 2 * HALO, LANES), F32),
                        pltpu.VMEM((tm, D), F32)],
        compiler_params=pltpu.CompilerParams(vmem_limit_bytes=VMEM_LIMIT),
        name="ffn",
    )(u2, u2, u2, x1, ga2, w_up, conv_w, conv_b.reshape(1, 2 * d_ff), w_down,
      ln_g.reshape(1, D), ln_b.reshape(1, D))


def kernel(x, c, positions, w_ada, b_ada, w_in, rec_lower_bound, rec_norm_g, w_rec_branch,
           attn_sink, w_attn_branch, w_out, ln1_g, ln1_b, w_up, conv_w, conv_b, w_down,
           ln2_g, ln2_b):
    B, S, D = x.shape
    assert w_ada.shape[0] == DEPTH and rec_lower_bound.shape[1] == DEPTH + 1
    tm_proj = min(1024, S)
    tq = min(1024, S)
    tm_ffn = min(1024, S)
    for l in range(DEPTH):
        mods = _ada(c, w_ada[l], b_ada[l])
        sh1, sc1, ga1, sh2, sc2, ga2 = [m.reshape(B, 1, D) for m in jnp.split(mods, 6, axis=-1)]
        (q, kf, kb, lf, lb, v, g, aq, ak, av, g_rec, g_att) = _proj(
            x, sc1, sh1, positions, w_in[l].astype(BF16), rec_lower_bound, tm_proj)
        orec = _rec(q, kf, kb, lf, lb, v, g, rec_norm_g[l])
        x1, u2 = _mix(x, ga1, sc2, sh2, aq, ak, av, orec, g_rec, g_att, attn_sink[l],
                      w_rec_branch[l].astype(BF16), w_attn_branch[l].astype(BF16),
                      w_out[l].astype(BF16), ln1_g[l], ln1_b[l], tq)
        x = _ffn(u2, x1, ga2, w_up[l].astype(BF16), conv_w[l], conv_b[l], w_down[l].astype(BF16),
                 ln2_g[l], ln2_b[l], tm_ffn)
    return x
```
